```python
import jax, jax.numpy as jnp
from jax import lax
import numpy as np

D_MODEL = 1024
BATCH = 32
SEQ = 2048
DEPTH = 4
DEC_BATCH = 8
DEC_SEQ = 32
PAST_LEN = 1024

CHUNK = 64
HEAD_DIM = 64
MIX_W = D_MODEL
MIX_HEADS = MIX_W // HEAD_DIM
N_MEM = 256
N_HEADS_MEM = 4
N_HEADS_A = MIX_HEADS - N_HEADS_MEM
N_HEADS_B = MIX_HEADS - N_HEADS_MEM
N_A_LAYERS = DEPTH // 2
N_B_LAYERS = DEPTH - N_A_LAYERS
BAND_CHUNKS = 8
WINDOW_B = BAND_CHUNKS * CHUNK
BAND = (BAND_CHUNKS + 1) * CHUNK
REL_CLIP = 128
SB_BLOCK = 128
D_FF = ((8 * D_MODEL // 3 + 255) // 256) * 256
EPS = 1e-6
NEG = -1e30

kernel_name = 'yoco_stickbreak_chunkband_streaming_encoder'


def rmsnorm(x, g):
    xf = x.astype(jnp.float32)
    y = xf * lax.rsqrt(jnp.mean(xf * xf, axis=-1, keepdims=True) + EPS)
    return (y * g.astype(jnp.float32)).astype(x.dtype)


def swiglu(x, w_gu, w_down):
    gate, up = jnp.split(x @ w_gu, 2, axis=-1)
    return (jax.nn.silu(gate) * up) @ w_down


def heads(t, n):
    return t.reshape(t.shape[:-1] + (n, HEAD_DIM))


def _sb_block(q, k, v, q_pos, k_pos):
    z = jnp.einsum('bqhd,bkhd->bhqk', q, k, preferred_element_type=jnp.float32) * (HEAD_DIM ** -0.5)
    vis = k_pos[None, :] < q_pos[:, None]
    log_keep = jnp.where(vis, jax.nn.log_sigmoid(-z), 0.0)
    after = lax.cumsum(log_keep, axis=log_keep.ndim - 1, reverse=True) - log_keep
    w = jnp.where(vis, jnp.exp(jax.nn.log_sigmoid(z) + after), 0.0)
    return jnp.einsum('bhqk,bkhd->bqhd', w.astype(v.dtype), v)


def stick_breaking(q, k, v, q_start):
    b, tq, h, d = q.shape
    blk = min(SB_BLOCK, tq)
    nb = tq // blk
    k_pos = jnp.arange(k.shape[1])
    qb = q.reshape(b, nb, blk, h, d).swapaxes(0, 1)

    def body(args):
        qi, i = args
        q_pos = q_start + i * blk + jnp.arange(blk)
        return _sb_block(qi, k, v, q_pos, k_pos)

    out = lax.map(body, (qb, jnp.arange(nb)))
    return out.swapaxes(0, 1).reshape(b, tq, h, d)


def _band_block(q, k, v, q_pos, k_pos, bias):
    s = jnp.einsum('bqhd,bkhd->bhqk', q, k, preferred_element_type=jnp.float32) * (HEAD_DIM ** -0.5)
    rel = jnp.clip(q_pos[:, None] - k_pos[None, :], -REL_CLIP, REL_CLIP) + REL_CLIP
    s = s + bias.astype(jnp.float32)[:, rel][None]
    qc = q_pos[:, None] // CHUNK
    kc = k_pos[None, :] // CHUNK
    vis = (k_pos[None, :] >= 0) & (kc <= qc) & (kc >= qc - BAND_CHUNKS)
    p = jax.nn.softmax(jnp.where(vis[None, None], s, NEG), axis=-1)
    return jnp.einsum('bhqk,bkhd->bqhd', p.astype(v.dtype), v)


def chunk_band_prompt(q, k, v, bias):
    b, t, h, d = q.shape
    nc = t // CHUNK
    pad = ((0, 0), (WINDOW_B, 0), (0, 0), (0, 0))
    kp = jnp.pad(k, pad)
    vp = jnp.pad(v, pad)
    qc = q.reshape(b, nc, CHUNK, h, d).swapaxes(0, 1)

    def body(args):
        qi, c = args
        start = c * CHUNK
        kb = lax.dynamic_slice_in_dim(kp, start, BAND, axis=1)
        vb = lax.dynamic_slice_in_dim(vp, start, BAND, axis=1)
        q_pos = start + jnp.arange(CHUNK)
        k_pos = start - WINDOW_B + jnp.arange(BAND)
        return _band_block(qi, kb, vb, q_pos, k_pos, bias)

    out = lax.map(body, (qc, jnp.arange(nc)))
    return out.swapaxes(0, 1).reshape(b, t, h, d)


def chunk_band_step(q, k_all, v_all, q_start, bias):
    tq = q.shape[1]
    tk = k_all.shape[1]
    q_pos = q_start + jnp.arange(tq)
    k_pos = q_start + tq - tk + jnp.arange(tk)
    return _band_block(q, k_all, v_all, q_pos, k_pos, bias)


def mem_attend(q, mk, mv):
    s = jnp.einsum('bqhd,bmhd->bhqm', q, mk, preferred_element_type=jnp.float32) * (HEAD_DIM ** -0.5)
    p = jax.nn.softmax(s, axis=-1)
    return jnp.einsum('bhqm,bmhd->bqhd', p.astype(mv.dtype), mv)


def setup_inputs(seed: int = 0) -> dict:
    key = jax.random.key(seed)
    ks = jax.random.split(key, 32)
    f32 = jnp.float32
    a_w = N_HEADS_A * HEAD_DIM
    b_w = N_HEADS_B * HEAD_DIM
    m_w = N_HEADS_MEM * HEAD_DIM
    wc = min(WINDOW_B, PAST_LEN)

    def nrm(k, shape, scale=1.0):
        return jax.random.normal(k, shape, f32) * scale

    def gain(k, shape):
        return 1.0 + 0.05 * jax.random.normal(k, shape, f32)

    dsc = D_MODEL ** -0.5
    return {
        'x_prompt': nrm(ks[0], (BATCH, SEQ, D_MODEL)),
        'x_sample': nrm(ks[1], (DEC_BATCH, DEC_SEQ, D_MODEL)),
        'cache_a_k': nrm(ks[2], (N_A_LAYERS, DEC_BATCH, PAST_LEN, N_HEADS_A, HEAD_DIM)),
        'cache_a_v': nrm(ks[3], (N_A_LAYERS, DEC_BATCH, PAST_LEN, N_HEADS_A, HEAD_DIM)),
        'cache_b_k': nrm(ks[4], (DEC_BATCH, wc, N_HEADS_B, HEAD_DIM)),
        'cache_b_v': nrm(ks[5], (DEC_BATCH, wc, N_HEADS_B, HEAD_DIM)),
        'cache_mem_k': nrm(ks[6], (DEPTH, DEC_BATCH, N_MEM, N_HEADS_MEM, HEAD_DIM)),
        'cache_mem_v': nrm(ks[7], (DEPTH, DEC_BATCH, N_MEM, N_HEADS_MEM, HEAD_DIM)),
        'mem_prompt': nrm(ks[8], (BATCH, N_MEM, D_MODEL)),
        'g_ff1': gain(ks[9], (DEPTH, D_MODEL)),
        'w_ff1_gu': nrm(ks[10], (DEPTH, D_MODEL, 2 * D_FF), dsc),
        'w_ff1_down': nrm(ks[11], (DEPTH, D_FF, D_MODEL), D_FF ** -0.5),
        'g_mix': gain(ks[12], (DEPTH, D_MODEL)),
        'w_in_a': nrm(ks[13], (N_A_LAYERS, D_MODEL, 3 * a_w + m_w), dsc),
        'w_in_b': nrm(ks[14], (N_B_LAYERS, D_MODEL, b_w + m_w), dsc),
        'w_out': nrm(ks[15], (DEPTH, MIX_W, D_MODEL), MIX_W ** -0.5),
        'g_mem': gain(ks[16], (DEPTH, D_MODEL)),
        'w_mem_kv': nrm(ks[17], (DEPTH, D_MODEL, 2 * m_w), dsc),
        'g_kv': gain(ks[18], (D_MODEL,)),
        'w_kv_b': nrm(ks[19], (D_MODEL, 2 * b_w), dsc),
        'rel_bias_b': nrm(ks[20], (N_B_LAYERS, N_HEADS_B, 2 * REL_CLIP + 1), 0.1),
        'g_ff2': gain(ks[21], (DEPTH, D_MODEL)),
        'w_ff2_gu': nrm(ks[22], (DEPTH, D_MODEL, 2 * D_FF), dsc),
        'w_ff2_down': nrm(ks[23], (DEPTH, D_FF, D_MODEL), D_FF ** -0.5),
        'g_final': gain(ks[24], (D_MODEL,)),
    }


def reference(x_prompt, x_sample, cache_a_k, cache_a_v, cache_b_k, cache_b_v, cache_mem_k, cache_mem_v,
              mem_prompt, g_ff1, w_ff1_gu, w_ff1_down, g_mix, w_in_a, w_in_b, w_out, g_mem, w_mem_kv,
              g_kv, w_kv_b, rel_bias_b, g_ff2, w_ff2_gu, w_ff2_down, g_final):
    a_w = N_HEADS_A * HEAD_DIM
    b_w = N_HEADS_B * HEAD_DIM
    m_w = N_HEADS_MEM * HEAD_DIM

    def run(x, mem_k, mem_v, past_a_k, past_a_v, past_b_k, past_b_v):
        bn, t, _ = x.shape
        has_past = past_a_k is not None
        q_start = past_a_k.shape[2] if has_past else 0
        a_k_rows, a_v_rows = [], []
        kb = vb = kb_all = vb_all = None
        for l in range(DEPTH):
            x = x + 0.5 * swiglu(rmsnorm(x, g_ff1[l]), w_ff1_gu[l], w_ff1_down[l])
            h = rmsnorm(x, g_mix[l])
            if l < N_A_LAYERS:
                proj = h @ w_in_a[l]
                q = heads(proj[..., :a_w], N_HEADS_A)
                k = heads(proj[..., a_w:2 * a_w], N_HEADS_A)
                v = heads(proj[..., 2 * a_w:3 * a_w], N_HEADS_A)
                qm = heads(proj[..., 3 * a_w:], N_HEADS_MEM)
                a_k_rows.append(k)
                a_v_rows.append(v)
                if has_past:
                    k = jnp.concatenate([past_a_k[l], k], axis=1)
                    v = jnp.concatenate([past_a_v[l], v], axis=1)
                o_tok = stick_breaking(q, k, v, q_start)
            else:
                j = l - N_A_LAYERS
                proj = h @ w_in_b[j]
                q = heads(proj[..., :b_w], N_HEADS_B)
                qm = heads(proj[..., b_w:], N_HEADS_MEM)
                if has_past:
                    o_tok = chunk_band_step(q, kb_all, vb_all, q_start, rel_bias_b[j])
                else:
                    o_tok = chunk_band_prompt(q, kb, vb, rel_bias_b[j])
            o_mem = mem_attend(qm, mem_k[l], mem_v[l])
            o = jnp.concatenate([o_tok.reshape(bn, t, -1), o_mem.reshape(bn, t, -1)], axis=-1)
            x = x + o @ w_out[l]
            x = x + 0.5 * swiglu(rmsnorm(x, g_ff2[l]), w_ff2_gu[l], w_ff2_down[l])
            if l == N_A_LAYERS - 1:
                kv = rmsnorm(x, g_kv) @ w_kv_b
                kb = heads(kv[..., :b_w], N_HEADS_B)
                vb = heads(kv[..., b_w:], N_HEADS_B)
                if has_past:
                    kb_all = jnp.concatenate([past_b_k, kb], axis=1)
                    vb_all = jnp.concatenate([past_b_v, vb], axis=1)
        return rmsnorm(x, g_final), jnp.stack(a_k_rows), jnp.stack(a_v_rows), kb, vb

    mk_list, mv_list = [], []
    for l in range(DEPTH):
        mkv = rmsnorm(mem_prompt, g_mem[l]) @ w_mem_kv[l]
        mk_list.append(heads(mkv[..., :m_w], N_HEADS_MEM))
        mv_list.append(heads(mkv[..., m_w:], N_HEADS_MEM))
    mem_k_prompt = jnp.stack(mk_list)
    mem_v_prompt = jnp.stack(mv_list)

    y_prompt, a_k_prompt, a_v_prompt, kb_p, vb_p = run(
        x_prompt, mem_k_prompt, mem_v_prompt, None, None, None, None)
    keep = min(WINDOW_B, kb_p.shape[1])
    b_k_prompt = kb_p[:, kb_p.shape[1] - keep:]
    b_v_prompt = vb_p[:, vb_p.shape[1] - keep:]

    y_sample, a_k_sample, a_v_sample, b_k_sample, b_v_sample = run(
        x_sample, cache_mem_k, cache_mem_v, cache_a_k, cache_a_v, cache_b_k, cache_b_v)

    return (y_prompt, y_sample, a_k_prompt, a_v_prompt, b_k_prompt, b_v_prompt, mem_k_prompt, mem_v_prompt,
            a_k_sample, a_v_sample, b_k_sample, b_v_sample)
```

```python
import functools

import numpy as np
import jax
import jax.numpy as jnp
from jax import lax
from jax.experimental import pallas as pl
from jax.experimental.pallas import tpu as pltpu

F32 = jnp.float32
BF16 = jnp.bfloat16

D_MODEL = 1024
HEAD_DIM = 64
CHUNK = 64
N_MEM = 256
N_HEADS_MEM = 4
N_HEADS_TOK = 12
TOK_W = N_HEADS_TOK * HEAD_DIM
MEM_W = N_HEADS_MEM * HEAD_DIM
BAND_CHUNKS = 8
WINDOW_B = BAND_CHUNKS * CHUNK
REL_CLIP = 128
EPS = 1e-6
NEG = -1e30
QK_SCALE = HEAD_DIM ** -0.5

LANES = 128
N_PAIRS_TOK = TOK_W // LANES
N_PAIRS_MEM = MEM_W // LANES
FF_CHUNK = 256
TOKEN_TILE = 512
SB_Q_BLOCK = 256
SB_K_BLOCK = 256
BAND_Q_BLOCK = 2 * CHUNK
BAND_WIN = WINDOW_B + BAND_Q_BLOCK
VMEM_LIMIT = 56 * 1024 * 1024


def _rms(x, g):
    return x * lax.rsqrt(jnp.mean(x * x, axis=-1, keepdims=True) + EPS) * g


def _dot(a, b):
    return jnp.dot(a, b, preferred_element_type=F32)


def _dot_nt(a, b):
    return lax.dot_general(a, b, (((1,), (1,)), ((), ())), preferred_element_type=F32)


def _head_split(x_pair):
    lane = lax.broadcasted_iota(jnp.int32, x_pair.shape, 1)
    zero = jnp.zeros_like(x_pair)
    return jnp.where(lane < HEAD_DIM, x_pair, zero), jnp.where(lane >= HEAD_DIM, x_pair, zero)


def _head_merge(o0, o1):
    lane = lax.broadcasted_iota(jnp.int32, o0.shape, 1)
    return jnp.where(lane < HEAD_DIM, o0, o1)


def _swiglu_residual(x_ref, g_ref, wgu_ref, wd_ref, h_ref, acc_ref):
    h_ref[...] = _rms(x_ref[...], g_ref[...]).astype(BF16)
    acc_ref[...] = jnp.zeros_like(acc_ref)

    def body(c, carry):
        gu = _dot(h_ref[...], wgu_ref[c])
        gate = gu[:, :FF_CHUNK]
        up = gu[:, FF_CHUNK:]
        act = gate * (1.0 / (1.0 + jnp.exp(-gate))) * up
        acc_ref[...] += _dot(act.astype(BF16), wd_ref[c])
        return carry

    lax.fori_loop(0, wgu_ref.shape[0], body, 0)
    return x_ref[...] + 0.5 * acc_ref[...]


def _softmax_pv(s, v):
    m = jnp.max(s, axis=-1, keepdims=True)
    e = jnp.exp(s - m)
    l = jnp.sum(e, axis=-1, keepdims=True)
    return _dot(e.astype(BF16), v) / l


def _pre_kernel(*refs, has_kv, n_streams):
    if has_kv:
        (x_ref, gff_ref, wgu_ref, wd_ref, gmix_ref, wq_ref, wk_ref, wv_ref, wqm_ref, mk_ref, mv_ref,
         xo_ref, q_ref, k_ref, v_ref, om_ref, h_ref, acc_ref) = refs
    else:
        (x_ref, gff_ref, wgu_ref, wd_ref, gmix_ref, wq_ref, wqm_ref, mk_ref, mv_ref,
         xo_ref, q_ref, om_ref, h_ref, acc_ref) = refs
    xo_ref[...] = _swiglu_residual(x_ref, gff_ref, wgu_ref, wd_ref, h_ref, acc_ref)
    h_ref[...] = _rms(xo_ref[...], gmix_ref[...]).astype(BF16)
    q_ref[...] = (_dot(h_ref[...], wq_ref[...]) * QK_SCALE).astype(BF16)
    if has_kv:
        k_ref[...] = _dot(h_ref[...], wk_ref[...])
        v_ref[...] = _dot(h_ref[...], wv_ref[...])
    qm = (_dot(h_ref[...], wqm_ref[...]) * QK_SCALE).astype(BF16)
    rows = x_ref.shape[0] // n_streams
    for s in range(n_streams):
        r0 = s * rows
        for p in range(N_PAIRS_MEM):
            c0 = p * LANES
            mk = mk_ref[s, :, c0:c0 + LANES].astype(BF16)
            mv = mv_ref[s, :, c0:c0 + LANES].astype(BF16)
            qa, qb = _head_split(qm[r0:r0 + rows, c0:c0 + LANES])
            oa = _softmax_pv(_dot_nt(qa, mk), mv)
            ob = _softmax_pv(_dot_nt(qb, mk), mv)
            om_ref[r0:r0 + rows, c0:c0 + LANES] = _head_merge(oa, ob).astype(BF16)


def _const_spec(shape):
    nd = len(shape)
    return pl.BlockSpec(shape, lambda i: (0,) * nd, pipeline_mode=pl.Buffered(1))


def _pre_call(x, seq_len, g_ff, wgu, wd, g_mix, w_in_parts, mem_k, mem_v):
    n = x.shape[0]
    tm = min(TOKEN_TILE, n)
    has_kv = len(w_in_parts) == 4
    if tm >= seq_len:
        n_streams = tm // seq_len
        mem_map = lambda i: (i, 0, 0)
    else:
        n_streams = 1
        tiles_per_seq = seq_len // tm
        mem_map = lambda i: (i // tiles_per_seq, 0, 0)
    row = lambda w: pl.BlockSpec((tm, w), lambda i: (i, 0))
    mem_spec = pl.BlockSpec((n_streams, N_MEM, MEM_W), mem_map)
    in_specs = [row(D_MODEL), _const_spec(g_ff.shape), _const_spec(wgu.shape), _const_spec(wd.shape),
                _const_spec(g_mix.shape)]
    in_specs += [_const_spec(w.shape) for w in w_in_parts]
    in_specs += [mem_spec, mem_spec]
    out_shape = [jax.ShapeDtypeStruct((n, D_MODEL), F32), jax.ShapeDtypeStruct((n, TOK_W), BF16)]
    out_specs = [row(D_MODEL), row(TOK_W)]
    if has_kv:
        out_shape += [jax.ShapeDtypeStruct((n, TOK_W), F32)] * 2
        out_specs += [row(TOK_W), row(TOK_W)]
    out_shape.append(jax.ShapeDtypeStruct((n, MEM_W), BF16))
    out_specs.append(row(MEM_W))
    return pl.pallas_call(
        functools.partial(_pre_kernel, has_kv=has_kv, n_streams=n_streams),
        grid=(n // tm,),
        in_specs=in_specs,
        out_specs=out_specs,
        out_shape=out_shape,
        scratch_shapes=[pltpu.VMEM((tm, D_MODEL), BF16), pltpu.VMEM((tm, D_MODEL), F32)],
        compiler_params=pltpu.CompilerParams(dimension_semantics=("arbitrary",),
                                             vmem_limit_bytes=VMEM_LIMIT),
        name="pre_kv" if has_kv else "pre_q",
    )(x, g_ff, wgu, wd, g_mix, *w_in_parts, mem_k, mem_v)


def _post_kernel(*refs, tail):
    if tail == "kv":
        (x_ref, ot_ref, om_ref, wot_ref, wom_ref, gff_ref, wgu_ref, wd_ref, gt_ref, wkb_ref, wvb_ref,
         xo_ref, kb_ref, vb_ref, h_ref, acc_ref) = refs
    elif tail == "final":
        (x_ref, ot_ref, om_ref, wot_ref, wom_ref, gff_ref, wgu_ref, wd_ref, gt_ref,
         xo_ref, h_ref, acc_ref) = refs
    else:
        (x_ref, ot_ref, om_ref, wot_ref, wom_ref, gff_ref, wgu_ref, wd_ref,
         xo_ref, h_ref, acc_ref) = refs
    xo_ref[...] = x_ref[...] + _dot(ot_ref[...], wot_ref[...]) + _dot(om_ref[...], wom_ref[...])
    xo_ref[...] = _swiglu_residual(xo_ref, gff_ref, wgu_ref, wd_ref, h_ref, acc_ref)
    if tail == "final":
        xo_ref[...] = _rms(xo_ref[...], gt_ref[...])
    if tail == "kv":
        h_ref[...] = _rms(xo_ref[...], gt_ref[...]).astype(BF16)
        kb_ref[...] = _dot(h_ref[...], wkb_ref[...])
        vb_ref[...] = _dot(h_ref[...], wvb_ref[...])


def _post_call(x, o_tok, o_mem, w_out_tok, w_out_mem, g_ff, wgu, wd, tail, tail_args):
    n = x.shape[0]
    tm = min(TOKEN_TILE, n)
    row = lambda w: pl.BlockSpec((tm, w), lambda i: (i, 0))
    consts = [w_out_tok, w_out_mem, g_ff, wgu, wd, *tail_args]
    in_specs = [row(D_MODEL), row(TOK_W), row(MEM_W)] + [_const_spec(c.shape) for c in consts]
    out_shape = [jax.ShapeDtypeStruct((n, D_MODEL), F32)]
    out_specs = [row(D_MODEL)]
    if tail == "kv":
        out_shape += [jax.ShapeDtypeStruct((n, TOK_W), F32)] * 2
        out_specs += [row(TOK_W), row(TOK_W)]
    return pl.pallas_call(
        functools.partial(_post_kernel, tail=tail),
        grid=(n // tm,),
        in_specs=in_specs,
        out_specs=out_specs,
        out_shape=out_shape,
        scratch_shapes=[pltpu.VMEM((tm, D_MODEL), BF16), pltpu.VMEM((tm, D_MODEL), F32)],
        compiler_params=pltpu.CompilerParams(dimension_semantics=("arbitrary",),
                                             vmem_limit_bytes=VMEM_LIMIT),
        name="post_" + tail,
    )(x, o_tok, o_mem, *consts)


def _memkv_kernel(m_ref, g_ref, wk_ref, wv_ref, k_ref, v_ref):
    h = _rms(m_ref[...], g_ref[0]).astype(BF16)
    k_ref[0] = _dot(h, wk_ref[0])
    v_ref[0] = _dot(h, wv_ref[0])


def _memkv_call(mem, g_mem, wk, wv):
    n = mem.shape[0]
    depth = g_mem.shape[0]
    tm = min(TOKEN_TILE, n)
    w_spec = pl.BlockSpec((1, D_MODEL, MEM_W), lambda l, i: (l, 0, 0))
    o_spec = pl.BlockSpec((1, tm, MEM_W), lambda l, i: (l, i, 0))
    return pl.pallas_call(
        _memkv_kernel,
        grid=(depth, n // tm),
        in_specs=[pl.BlockSpec((tm, D_MODEL), lambda l, i: (i, 0)),
                  pl.BlockSpec((1, 1, D_MODEL), lambda l, i: (l, 0, 0)), w_spec, w_spec],
        out_specs=[o_spec, o_spec],
        out_shape=[jax.ShapeDtypeStruct((depth, n, MEM_W), F32)] * 2,
        compiler_params=pltpu.CompilerParams(dimension_semantics=("arbitrary", "arbitrary")),
        name="mem_kv",
    )(mem, g_mem, wk, wv)


def _sb_kernel(q_ref, k_ref, v_ref, o_ref, kb_ref, vb_ref, *, q_start):
    i = pl.program_id(2)
    tq = q_ref.shape[0]
    tk = SB_K_BLOCK

    @pl.when(i == 0)
    def _():
        kb_ref[...] = k_ref[...].astype(BF16)
        vb_ref[...] = v_ref[...].astype(BF16)

    q_heads = _head_split(q_ref[...])
    q0 = q_start + i * tq
    q_pos = q0 + lax.broadcasted_iota(jnp.int32, (tq, tk), 0)
    k_off = lax.broadcasted_iota(jnp.int32, (tq, tk), 1)
    u = jnp.where(lax.broadcasted_iota(jnp.int32, (tk, tk), 0)
                  >= lax.broadcasted_iota(jnp.int32, (tk, tk), 1), 1.0, 0.0).astype(BF16)
    n_blocks = jnp.minimum((q0 + tq - 2) // tk + 1, kb_ref.shape[0] // tk)

    def body(t, carry):
        ks = pl.multiple_of((n_blocks - 1 - t) * tk, tk)
        kblk = kb_ref[pl.ds(ks, tk), :]
        vblk = vb_ref[pl.ds(ks, tk), :]
        vis = (ks + k_off) < q_pos
        out = []
        for hh in range(2):
            acc, run = carry[hh]
            z = _dot_nt(q_heads[hh], kblk)
            sp = jnp.maximum(z, 0.0) + jnp.log(1.0 + jnp.exp(-jnp.abs(z)))
            sp = jnp.where(vis, sp, 0.0)
            hi = sp.astype(BF16)
            lo = (sp - hi.astype(F32)).astype(BF16)
            csum = _dot(hi, u) + _dot(lo, u) + run
            w = jnp.where(vis, jnp.exp(z - csum), 0.0)
            acc = acc + _dot(w.astype(BF16), vblk)
            out.append((acc, csum[:, 0:1]))
        return tuple(out)

    zero = (jnp.zeros((tq, LANES), F32), jnp.zeros((tq, 1), F32))
    (acc0, _), (acc1, _) = lax.fori_loop(0, n_blocks, body, (zero, zero))
    o_ref[...] = _head_merge(acc0, acc1).astype(BF16)


def _sb_call(q, k, v, q_start):
    b, t, _ = q.shape
    tk_all = k.shape[1]
    tq = min(SB_Q_BLOCK, t)
    q_spec = pl.BlockSpec((None, tq, LANES), lambda bi, p, i: (bi, i, p))
    kv_spec = pl.BlockSpec((None, tk_all, LANES), lambda bi, p, i: (bi, 0, p))
    return pl.pallas_call(
        functools.partial(_sb_kernel, q_start=q_start),
        grid=(b, N_PAIRS_TOK, t // tq),
        in_specs=[q_spec, kv_spec, kv_spec],
        out_specs=q_spec,
        out_shape=jax.ShapeDtypeStruct((b, t, TOK_W), BF16),
        scratch_shapes=[pltpu.VMEM((tk_all, LANES), BF16), pltpu.VMEM((tk_all, LANES), BF16)],
        compiler_params=pltpu.CompilerParams(dimension_semantics=("arbitrary",) * 3),
        name="stick_breaking",
    )(q, k, v)


def _band_bias(rel_bias, q_pos, k_pos):
    rel = np.clip(q_pos[:, None] - k_pos[None, :], -REL_CLIP, REL_CLIP) + REL_CLIP
    qc = q_pos[:, None] // CHUNK
    kc = k_pos[None, :] // CHUNK
    vis = (kc <= qc) & (kc >= qc - BAND_CHUNKS)
    return jnp.where(vis[None], rel_bias[:, rel], NEG)


def _band_heads(q_pair, kw, vw, bm_ref, valid):
    out = []
    for hh, qh in enumerate(_head_split(q_pair)):
        s = _dot_nt(qh, kw) + bm_ref[hh]
        if valid is not None:
            s = jnp.where(valid, s, NEG)
        out.append(_softmax_pv(s, vw))
    return _head_merge(*out).astype(BF16)


def _band_prompt_kernel(q_ref, k_ref, v_ref, bm_ref, o_ref, kp_ref, vp_ref):
    i = pl.program_id(2)
    tq = q_ref.shape[0]

    @pl.when(i == 0)
    def _():
        pad = jnp.zeros((WINDOW_B, LANES), BF16)
        kp_ref[0:WINDOW_B, :] = pad
        vp_ref[0:WINDOW_B, :] = pad
        kp_ref[WINDOW_B:, :] = k_ref[...].astype(BF16)
        vp_ref[WINDOW_B:, :] = v_ref[...].astype(BF16)

    start = pl.multiple_of(i * tq, tq)
    kw = kp_ref[pl.ds(start, BAND_WIN), :]
    vw = vp_ref[pl.ds(start, BAND_WIN), :]
    valid = lax.broadcasted_iota(jnp.int32, (tq, BAND_WIN), 1) >= WINDOW_B - start
    o_ref[...] = _band_heads(q_ref[...], kw, vw, bm_ref, valid)


def _band_prompt_call(q, k, v, rel_bias):
    b, t, _ = q.shape
    tq = BAND_Q_BLOCK
    bm = _band_bias(rel_bias, WINDOW_B + np.arange(tq), np.arange(BAND_WIN))
    q_spec = pl.BlockSpec((None, tq, LANES), lambda bi, p, i: (bi, i, p))
    kv_spec = pl.BlockSpec((None, t, LANES), lambda bi, p, i: (bi, 0, p))
    return pl.pallas_call(
        _band_prompt_kernel,
        grid=(b, N_PAIRS_TOK, t // tq),
        in_specs=[q_spec, kv_spec, kv_spec,
                  pl.BlockSpec((2, tq, BAND_WIN), lambda bi, p, i: (p, 0, 0))],
        out_specs=q_spec,
        out_shape=jax.ShapeDtypeStruct((b, t, TOK_W), BF16),
        scratch_shapes=[pltpu.VMEM((WINDOW_B + t, LANES), BF16)] * 2,
        compiler_params=pltpu.CompilerParams(dimension_semantics=("arbitrary",) * 3),
        name="band_prompt",
    )(q, k, v, bm)


def _band_step_kernel(q_ref, k_ref, v_ref, bm_ref, o_ref):
    o_ref[...] = _band_heads(q_ref[...], k_ref[...].astype(BF16), v_ref[...].astype(BF16), bm_ref, None)


def _band_step_call(q, k_all, v_all, rel_bias, q_start):
    b, t, _ = q.shape
    tk = k_all.shape[1]
    bm = _band_bias(rel_bias, q_start + np.arange(t), q_start + t - tk + np.arange(tk))
    q_spec = pl.BlockSpec((None, t, LANES), lambda bi, p: (bi, 0, p))
    kv_spec = pl.BlockSpec((None, tk, LANES), lambda bi, p: (bi, 0, p))
    return pl.pallas_call(
        _band_step_kernel,
        grid=(b, N_PAIRS_TOK),
        in_specs=[q_spec, kv_spec, kv_spec, pl.BlockSpec((2, t, tk), lambda bi, p: (p, 0, 0))],
        out_specs=q_spec,
        out_shape=jax.ShapeDtypeStruct((b, t, TOK_W), BF16),
        compiler_params=pltpu.CompilerParams(dimension_semantics=("arbitrary",) * 2),
        name="band_step",
    )(q, k_all, v_all, bm)


def _prep_ffn(w_gu, w_down):
    d, two_f = w_gu.shape
    f = two_f // 2
    nc = f // FF_CHUNK
    gate = w_gu[:, :f].reshape(d, nc, FF_CHUNK)
    up = w_gu[:, f:].reshape(d, nc, FF_CHUNK)
    wgu = jnp.concatenate([gate, up], axis=-1).transpose(1, 0, 2).astype(BF16)
    return wgu, w_down.reshape(nc, FF_CHUNK, d).astype(BF16)


def _split_cols(w, widths):
    out, c = [], 0
    for wd in widths:
        out.append(w[:, c:c + wd].astype(BF16))
        c += wd
    return out


def _prep_layers(params):
    depth = params["g_ff1"].shape[0]
    n_a = params["w_in_a"].shape[0]
    layers = []
    for l in range(depth):
        lay = {name: params[name][l].reshape(1, D_MODEL) for name in ("g_ff1", "g_mix", "g_ff2")}
        lay["ff1"] = _prep_ffn(params["w_ff1_gu"][l], params["w_ff1_down"][l])
        lay["ff2"] = _prep_ffn(params["w_ff2_gu"][l], params["w_ff2_down"][l])
        if l < n_a:
            lay["w_in"] = _split_cols(params["w_in_a"][l], (TOK_W, TOK_W, TOK_W, MEM_W))
        else:
            lay["w_in"] = _split_cols(params["w_in_b"][l - n_a], (TOK_W, MEM_W))
            lay["rel_bias"] = params["rel_bias_b"][l - n_a]
        lay["w_out"] = (params["w_out"][l][:TOK_W].astype(BF16), params["w_out"][l][TOK_W:].astype(BF16))
        if l == n_a - 1:
            lay["tail"] = "kv"
            lay["tail_args"] = [params["g_kv"].reshape(1, D_MODEL),
                                *_split_cols(params["w_kv_b"], (TOK_W, TOK_W))]
        elif l == depth - 1:
            lay["tail"], lay["tail_args"] = "final", [params["g_final"].reshape(1, D_MODEL)]
        else:
            lay["tail"], lay["tail_args"] = "none", []
        layers.append(lay)
    return layers, n_a


def _run(x, mem_k, mem_v, past_a_k, past_a_v, past_b_k, past_b_v, layers, n_a):
    bn, t, _ = x.shape
    n = bn * t
    has_past = past_a_k is not None
    q_start = past_a_k.shape[2] if has_past else 0
    x = x.reshape(n, D_MODEL)
    a_k_rows, a_v_rows = [], []
    kb = vb = None
    for l, lay in enumerate(layers):
        pre_args = (lay["g_ff1"], *lay["ff1"], lay["g_mix"], lay["w_in"], mem_k[l], mem_v[l])
        if l < n_a:
            x, q, k, v, o_mem = _pre_call(x, t, *pre_args)
            k = k.reshape(bn, t, TOK_W)
            v = v.reshape(bn, t, TOK_W)
            a_k_rows.append(k)
            a_v_rows.append(v)
            if has_past:
                t_all = q_start + t
                pad = (-t_all) % SB_K_BLOCK
                zeros = jnp.zeros((bn, pad, TOK_W), F32)
                k = jnp.concatenate([past_a_k[l].reshape(bn, q_start, TOK_W), k, zeros], axis=1)
                v = jnp.concatenate([past_a_v[l].reshape(bn, q_start, TOK_W), v, zeros], axis=1)
            o_tok = _sb_call(q.reshape(bn, t, TOK_W), k, v, q_start)
        else:
            x, q, o_mem = _pre_call(x, t, *pre_args)
            q = q.reshape(bn, t, TOK_W)
            if has_past:
                k_all = jnp.concatenate([past_b_k.reshape(bn, -1, TOK_W), kb], axis=1)
                v_all = jnp.concatenate([past_b_v.reshape(bn, -1, TOK_W), vb], axis=1)
                o_tok = _band_step_call(q, k_all, v_all, lay["rel_bias"], q_start)
            else:
                o_tok = _band_prompt_call(q, kb, vb, lay["rel_bias"])
        outs = _post_call(x, o_tok.reshape(n, TOK_W), o_mem, *lay["w_out"], lay["g_ff2"], *lay["ff2"],
                          lay["tail"], lay["tail_args"])
        x = outs[0]
        if lay["tail"] == "kv":
            kb = outs[1].reshape(bn, t, TOK_W)
            vb = outs[2].reshape(bn, t, TOK_W)
    heads = lambda a: a.reshape(a.shape[:-1] + (N_HEADS_TOK, HEAD_DIM))
    return (x.reshape(bn, t, D_MODEL), heads(jnp.stack(a_k_rows)), heads(jnp.stack(a_v_rows)),
            heads(kb), heads(vb))


def kernel(x_prompt, x_sample, cache_a_k, cache_a_v, cache_b_k, cache_b_v, cache_mem_k, cache_mem_v,
           mem_prompt, g_ff1, w_ff1_gu, w_ff1_down, g_mix, w_in_a, w_in_b, w_out, g_mem, w_mem_kv,
           g_kv, w_kv_b, rel_bias_b, g_ff2, w_ff2_gu, w_ff2_down, g_final):
    params = dict(g_ff1=g_ff1, w_ff1_gu=w_ff1_gu, w_ff1_down=w_ff1_down, g_mix=g_mix, w_in_a=w_in_a,
                  w_in_b=w_in_b, w_out=w_out, g_kv=g_kv, w_kv_b=w_kv_b, rel_bias_b=rel_bias_b,
                  g_ff2=g_ff2, w_ff2_gu=w_ff2_gu, w_ff2_down=w_ff2_down, g_final=g_final)
    depth = g_mem.shape[0]
    bn, n_mem, _ = mem_prompt.shape

    mk, mv = _memkv_call(mem_prompt.reshape(bn * n_mem, D_MODEL), g_mem.reshape(depth, 1, D_MODEL),
                         w_mem_kv[:, :, :MEM_W].astype(BF16), w_mem_kv[:, :, MEM_W:].astype(BF16))
    mk = mk.reshape(depth, bn, n_mem, MEM_W)
    mv = mv.reshape(depth, bn, n_mem, MEM_W)
    mem_heads = lambda a: a.reshape(a.shape[:-1] + (N_HEADS_MEM, HEAD_DIM))

    layers, n_a = _prep_layers(params)
    y_prompt, a_k_prompt, a_v_prompt, kb_p, vb_p = _run(x_prompt, mk, mv, None, None, None, None, layers, n_a)
    keep = min(WINDOW_B, kb_p.shape[1])
    b_k_prompt = kb_p[:, kb_p.shape[1] - keep:]
    b_v_prompt = vb_p[:, vb_p.shape[1] - keep:]

    dn = cache_mem_k.shape[1]
    y_sample, a_k_sample, a_v_sample, b_k_sample, b_v_sample = _run(
        x_sample, cache_mem_k.reshape(depth, dn, n_mem, MEM_W), cache_mem_v.reshape(depth, dn, n_mem, MEM_W),
        cache_a_k, cache_a_v, cache_b_k, cache_b_v, layers, n_a)

    return (y_prompt, y_sample, a_k_prompt, a_v_prompt, b_k_prompt, b_v_prompt, mem_heads(mk), mem_heads(mv),
            a_k_sample, a_v_sample, b_k_sample, b_v_sample)
```

```python
import functools

import numpy as np
import jax
import jax.numpy as jnp
from jax import lax
from jax.experimental import pallas as pl
from jax.experimental.pallas import tpu as pltpu

F32 = jnp.float32
BF16 = jnp.bfloat16

D_MODEL = 1024
HEAD_DIM = 64
CHUNK = 64
N_MEM = 256
N_HEADS_MEM = 4
N_HEADS_TOK = 12
TOK_W = N_HEADS_TOK * HEAD_DIM
MEM_W = N_HEADS_MEM * HEAD_DIM
BAND_CHUNKS = 8
WINDOW_B = BAND_CHUNKS * CHUNK
REL_CLIP = 128
EPS = 1e-6
NEG = -1e30
QK_SCALE = HEAD_DIM ** -0.5
LOG2E = 1.4426950408889634

LANES = 128
N_PAIRS_TOK = TOK_W // LANES
N_PAIRS_MEM = MEM_W // LANES
FF_CHUNK = 256
TOKEN_TILE = 512
SB_Q_BLOCK = 256
SB_K_BLOCK = 256
SB_LANES = 2 * LANES
SB_EXIT = 160.0
BAND_Q_BLOCK = 2 * CHUNK
BAND_WIN = WINDOW_B + BAND_Q_BLOCK
BAND_EDGE = WINDOW_B // BAND_Q_BLOCK
VMEM_LIMIT = 56 * 1024 * 1024


def _rms(x, g):
    return x * lax.rsqrt(jnp.mean(x * x, axis=-1, keepdims=True) + EPS) * g


def _dot(a, b):
    return jnp.dot(a, b, preferred_element_type=F32)


def _dot_nt(a, b):
    return lax.dot_general(a, b, (((1,), (1,)), ((), ())), preferred_element_type=F32)


def _neg_abs(x):
    return pltpu.bitcast(pltpu.bitcast(x, jnp.uint32) | jnp.uint32(0x80000000), F32)


def _head_split(x, n_heads):
    head = jnp.right_shift(lax.broadcasted_iota(jnp.int32, x.shape, 1), HEAD_DIM.bit_length() - 1)
    zero = jnp.zeros_like(x)
    return [jnp.where(head == h, x, zero) for h in range(n_heads)]


def _head_merge(o0, o1):
    lane = lax.broadcasted_iota(jnp.int32, o0.shape, 1)
    return jnp.where(lane < HEAD_DIM, o0, o1)


def _swiglu_residual(x_ref, g_ref, wgu_ref, wd_ref, h_ref, a_ref):
    h_ref[...] = _rms(x_ref[...], g_ref[...]).astype(BF16)
    f = wd_ref.shape[0]
    for lo in range(0, f, FF_CHUNK):
        gate = _dot(h_ref[...], wgu_ref[:, lo:lo + FF_CHUNK])
        up = _dot(h_ref[...], wgu_ref[:, f + lo:f + lo + FF_CHUNK])
        a_ref[:, lo:lo + FF_CHUNK] = (gate * (1.0 / (1.0 + jnp.exp(-gate))) * up).astype(BF16)
    return x_ref[...] + 0.5 * _dot(a_ref[...], wd_ref[...])


def _softmax_pv(s, v):
    m = jnp.max(s, axis=-1, keepdims=True)
    e = jnp.exp(s - m)
    l = jnp.sum(e, axis=-1, keepdims=True)
    return _dot(e.astype(BF16), v) / l


def _pre_kernel(*refs, has_kv, n_alias, n_streams):
    if has_kv:
        (x_ref, gff_ref, wgu_ref, wd_ref, gmix_ref, wq_ref, wk_ref, wv_ref, wqm_ref, mk_ref, mv_ref) = refs[:11]
        xo_ref, q_ref, k_ref, v_ref, om_ref, h_ref, a_ref = refs[11 + n_alias:]
    else:
        (x_ref, gff_ref, wgu_ref, wd_ref, gmix_ref, wq_ref, wqm_ref, mk_ref, mv_ref,
         xo_ref, q_ref, om_ref, h_ref, a_ref) = refs
    xo_ref[...] = _swiglu_residual(x_ref, gff_ref, wgu_ref, wd_ref, h_ref, a_ref)
    h_ref[...] = _rms(xo_ref[...], gmix_ref[...]).astype(BF16)
    q_ref[...] = (_dot(h_ref[...], wq_ref[...]) * QK_SCALE).astype(BF16)
    if has_kv:
        k_ref[...] = _dot(h_ref[...], wk_ref[...])
        v_ref[...] = _dot(h_ref[...], wv_ref[...])
    qm = (_dot(h_ref[...], wqm_ref[...]) * QK_SCALE).astype(BF16)
    rows = x_ref.shape[0] // n_streams
    for s in range(n_streams):
        r0 = s * rows
        for p in range(N_PAIRS_MEM):
            c0 = p * LANES
            mk = mk_ref[s, :, c0:c0 + LANES].astype(BF16)
            mv = mv_ref[s, :, c0:c0 + LANES].astype(BF16)
            qa, qb = _head_split(qm[r0:r0 + rows, c0:c0 + LANES], 2)
            oa = _softmax_pv(_dot_nt(qa, mk), mv)
            ob = _softmax_pv(_dot_nt(qb, mk), mv)
            om_ref[r0:r0 + rows, c0:c0 + LANES] = _head_merge(oa, ob).astype(BF16)


def _const_spec(shape):
    nd = len(shape)
    return pl.BlockSpec(shape, lambda i: (0,) * nd, pipeline_mode=pl.Buffered(1))


def _token_tile(n):
    tm = min(TOKEN_TILE, n)
    assert n % tm == 0
    return tm


def _pre_call(x, seq_len, g_ff, wgu, wd, g_mix, w_in_parts, mem_k, mem_v, kv_slot=None):
    n = x.shape[0]
    tm = _token_tile(n)
    has_kv = kv_slot is not None
    if tm >= seq_len:
        assert tm % seq_len == 0
        n_streams = tm // seq_len
        mem_map = lambda i: (i, 0, 0)
    else:
        assert seq_len % tm == 0
        n_streams = 1
        tiles_per_seq = seq_len // tm
        mem_map = lambda i: (i // tiles_per_seq, 0, 0)
    row = lambda w: pl.BlockSpec((tm, w), lambda i: (i, 0))
    mem_spec = pl.BlockSpec((n_streams, N_MEM, MEM_W), mem_map)
    args = [x, g_ff, wgu, wd, g_mix, *w_in_parts, mem_k, mem_v]
    in_specs = [row(D_MODEL)] + [_const_spec(a.shape) for a in args[1:-2]] + [mem_spec, mem_spec]
    out_shape = [jax.ShapeDtypeStruct((n, D_MODEL), F32), jax.ShapeDtypeStruct((n, TOK_W), BF16)]
    out_specs = [row(D_MODEL), row(TOK_W)]
    aliases = {}
    n_alias = 0
    if has_kv:
        layer, n_layers, k_stack, v_stack = kv_slot
        out_shape += [jax.ShapeDtypeStruct((n_layers, n, TOK_W), F32)] * 2
        out_specs += [pl.BlockSpec((None, tm, TOK_W), lambda i: (layer, i, 0))] * 2
        if k_stack is not None:
            n_alias = 2
            aliases = {len(args): 2, len(args) + 1: 3}
            args += [k_stack, v_stack]
            in_specs += [pl.BlockSpec(memory_space=pl.ANY)] * 2
    out_shape.append(jax.ShapeDtypeStruct((n, MEM_W), BF16))
    out_specs.append(row(MEM_W))
    d_ff = wd.shape[0]
    return pl.pallas_call(
        functools.partial(_pre_kernel, has_kv=has_kv, n_alias=n_alias, n_streams=n_streams),
        grid=(n // tm,),
        in_specs=in_specs,
        out_specs=out_specs,
        out_shape=out_shape,
        input_output_aliases=aliases,
        scratch_shapes=[pltpu.VMEM((tm, D_MODEL), BF16), pltpu.VMEM((tm, d_ff), BF16)],
        compiler_params=pltpu.CompilerParams(dimension_semantics=("arbitrary",),
                                             vmem_limit_bytes=VMEM_LIMIT),
        name="pre_kv" if has_kv else "pre_q",
    )(*args)


def _post_kernel(*refs, tail):
    if tail == "kv":
        (x_ref, ot_ref, om_ref, wot_ref, wom_ref, gff_ref, wgu_ref, wd_ref, gt_ref, wkb_ref, wvb_ref,
         xo_ref, kt_ref, vt_ref, k16_ref, v16_ref, h_ref, a_ref) = refs
    elif tail == "final":
        (x_ref, ot_ref, om_ref, wot_ref, wom_ref, gff_ref, wgu_ref, wd_ref, gt_ref,
         xo_ref, h_ref, a_ref) = refs
    else:
        (x_ref, ot_ref, om_ref, wot_ref, wom_ref, gff_ref, wgu_ref, wd_ref,
         xo_ref, h_ref, a_ref) = refs
    xo_ref[...] = x_ref[...] + _dot(ot_ref[...], wot_ref[...]) + _dot(om_ref[...], wom_ref[...])
    xo_ref[...] = _swiglu_residual(xo_ref, gff_ref, wgu_ref, wd_ref, h_ref, a_ref)
    if tail == "final":
        xo_ref[...] = _rms(xo_ref[...], gt_ref[...])
    if tail == "kv":
        h_ref[...] = _rms(xo_ref[...], gt_ref[...]).astype(BF16)
        kb = _dot(h_ref[...], wkb_ref[...])
        vb = _dot(h_ref[...], wvb_ref[...])
        kt_ref[...] = kb
        vt_ref[...] = vb
        k16_ref[...] = kb.astype(BF16)
        v16_ref[...] = vb.astype(BF16)


def _post_call(x, seq_len, o_tok, o_mem, w_out_tok, w_out_mem, g_ff, wgu, wd, tail, tail_args):
    n = x.shape[0]
    tm = _token_tile(n)
    row = lambda w: pl.BlockSpec((tm, w), lambda i: (i, 0))
    consts = [w_out_tok, w_out_mem, g_ff, wgu, wd, *tail_args]
    in_specs = [row(D_MODEL), row(TOK_W), row(MEM_W)] + [_const_spec(c.shape) for c in consts]
    out_shape = [jax.ShapeDtypeStruct((n, D_MODEL), F32)]
    out_specs = [row(D_MODEL)]
    if tail == "kv":
        keep = min(WINDOW_B, seq_len)
        if keep == seq_len:
            tail_rows, tail_spec = n, row(TOK_W)
        else:
            assert keep == tm and seq_len % tm == 0
            tiles_per_seq = seq_len // tm
            tail_rows = (n // seq_len) * keep
            tail_spec = pl.BlockSpec((tm, TOK_W), lambda i: (i // tiles_per_seq, 0))
        out_shape += [jax.ShapeDtypeStruct((tail_rows, TOK_W), F32)] * 2
        out_shape += [jax.ShapeDtypeStruct((n, TOK_W), BF16)] * 2
        out_specs += [tail_spec, tail_spec, row(TOK_W), row(TOK_W)]
    d_ff = wd.shape[0]
    return pl.pallas_call(
        functools.partial(_post_kernel, tail=tail),
        grid=(n // tm,),
        in_specs=in_specs,
        out_specs=out_specs,
        out_shape=out_shape,
        scratch_shapes=[pltpu.VMEM((tm, D_MODEL), BF16), pltpu.VMEM((tm, d_ff), BF16)],
        compiler_params=pltpu.CompilerParams(dimension_semantics=("arbitrary",),
                                             vmem_limit_bytes=VMEM_LIMIT),
        name="post_" + tail,
    )(x, o_tok, o_mem, *consts)


def _memkv_kernel(m_ref, g_ref, wk_ref, wv_ref, k_ref, v_ref):
    h = _rms(m_ref[...], g_ref[0]).astype(BF16)
    k_ref[0] = _dot(h, wk_ref[0])
    v_ref[0] = _dot(h, wv_ref[0])


def _memkv_call(mem, g_mem, wk, wv):
    n = mem.shape[0]
    depth = g_mem.shape[0]
    tm = _token_tile(n)
    w_spec = pl.BlockSpec((1, D_MODEL, MEM_W), lambda l, i: (l, 0, 0))
    o_spec = pl.BlockSpec((1, tm, MEM_W), lambda l, i: (l, i, 0))
    return pl.pallas_call(
        _memkv_kernel,
        grid=(depth, n // tm),
        in_specs=[pl.BlockSpec((tm, D_MODEL), lambda l, i: (i, 0)),
                  pl.BlockSpec((1, 1, D_MODEL), lambda l, i: (l, 0, 0)), w_spec, w_spec],
        out_specs=[o_spec, o_spec],
        out_shape=[jax.ShapeDtypeStruct((depth, n, MEM_W), F32)] * 2,
        compiler_params=pltpu.CompilerParams(dimension_semantics=("arbitrary", "arbitrary")),
        name="mem_kv",
    )(mem, g_mem, wk, wv)


def _sb_kernel(q_ref, k_ref, v_ref, o_ref, kb_ref, vb_ref, qs_ref, acc_ref, *, q_start):
    i = pl.program_id(2)
    tq, width = q_ref.shape
    tk = SB_K_BLOCK
    n_heads = width // HEAD_DIM

    @pl.when(i == 0)
    def _():
        kb_ref[...] = k_ref[...].astype(BF16)
        vb_ref[...] = v_ref[...].astype(BF16)

    for h, qh in enumerate(_head_split(q_ref[...], n_heads)):
        qs_ref[h * tq:(h + 1) * tq, :] = qh
    rows = n_heads * tq
    q0 = q_start + i * tq
    u_after = jnp.where(lax.broadcasted_iota(jnp.int32, (tk, tk), 0)
                        > lax.broadcasted_iota(jnp.int32, (tk, tk), 1), 1.0, 0.0).astype(BF16)
    n_blocks = jnp.minimum((q0 + tq - 2) // tk + 1, kb_ref.shape[0] // tk)

    def sweep_block(t, run, newest):
        ks = pl.multiple_of((n_blocks - 1 - t) * tk, tk)
        z = _dot_nt(qs_ref[...], kb_ref[pl.ds(ks, tk), :]) * LOG2E
        sp = jnp.maximum(z, 0.0) + jnp.log2(1.0 + jnp.exp2(_neg_abs(z)))
        if newest:
            q_pos = q0 + (lax.broadcasted_iota(jnp.int32, (rows, tk), 0) & (tq - 1))
            vis = ks + lax.broadcasted_iota(jnp.int32, (rows, tk), 1) < q_pos
            sp = jnp.where(vis, sp, 0.0)
        sp16 = sp.astype(BF16)
        after = _dot(sp16, u_after)
        if newest:
            w = jnp.where(vis, jnp.exp2((z - sp) - after), 0.0)
        else:
            w = jnp.exp2((z - sp) - after - run)
        w = w.astype(BF16)
        for p in range(n_heads // 2):
            pair_rows = slice(2 * p * tq, (2 * p + 2) * tq)
            pv = _dot(w[pair_rows], vb_ref[pl.ds(ks, tk), p * LANES:(p + 1) * LANES])
            if newest:
                acc_ref[pair_rows] = pv
            else:
                acc_ref[pair_rows] += pv
        return run + after[:, 0:1] + sp16[:, 0:1].astype(F32)

    def unfinished(run):
        return (jnp.min(run) < SB_EXIT).astype(jnp.int32)

    def cond(carry):
        t, go, _ = carry
        return jnp.logical_and(t < n_blocks, go > 0)

    def body(carry):
        t, _, run = carry
        run = sweep_block(t, run, False)
        return t + 1, unfinished(run), run

    run = sweep_block(0, jnp.zeros((rows, 1), F32), True)
    lax.while_loop(cond, body, (jnp.int32(1), unfinished(run), run))
    for p in range(n_heads // 2):
        r0 = 2 * p * tq
        o_ref[:, p * LANES:(p + 1) * LANES] = _head_merge(
            acc_ref[r0:r0 + tq], acc_ref[r0 + tq:r0 + 2 * tq]).astype(BF16)


def _sb_call(q, k, v, q_start, layer=None):
    b, t, _ = q.shape
    tk_all = k.shape[-2]
    tq = min(SB_Q_BLOCK, t)
    assert t % tq == 0 and SB_K_BLOCK % tq == 0 and q_start % tq == 0 and tk_all % SB_K_BLOCK == 0
    assert tq & (tq - 1) == 0
    q_spec = pl.BlockSpec((None, tq, SB_LANES), lambda bi, g, i: (bi, i, g))
    if layer is None:
        kv_spec = pl.BlockSpec((None, tk_all, SB_LANES), lambda bi, g, i: (bi, 0, g))
    else:
        kv_spec = pl.BlockSpec((None, None, tk_all, SB_LANES), lambda bi, g, i: (layer, bi, 0, g))
    return pl.pallas_call(
        functools.partial(_sb_kernel, q_start=q_start),
        grid=(b, TOK_W // SB_LANES, t // tq),
        in_specs=[q_spec, kv_spec, kv_spec],
        out_specs=q_spec,
        out_shape=jax.ShapeDtypeStruct((b, t, TOK_W), BF16),
        scratch_shapes=[pltpu.VMEM((tk_all, SB_LANES), BF16), pltpu.VMEM((tk_all, SB_LANES), BF16),
                        pltpu.VMEM((SB_LANES // HEAD_DIM * tq, SB_LANES), BF16),
                        pltpu.VMEM((SB_LANES // HEAD_DIM * tq, LANES), F32)],
        compiler_params=pltpu.CompilerParams(dimension_semantics=("arbitrary",) * 3),
        name="stick_breaking",
    )(q, k, v)


def _toeplitz(line, rows, cols):
    period = rows + cols
    e = jnp.roll(jnp.pad(line, ((0, 0), (0, 1))), -(rows - 1), axis=1)
    tiled = jnp.tile(e, (1, rows))[:, :rows * (period - 1)]
    return tiled.reshape(line.shape[0], rows, period - 1)[:, :, :cols]


def _band_bias(rel_bias, q_pos0, n_q, k_pos0, n_k):
    q_pos = q_pos0 + np.arange(n_q)
    k_pos = k_pos0 + np.arange(n_k)
    dist = (q_pos0 - k_pos0) + (n_q - 1) - np.arange(n_q + n_k - 1)
    line = rel_bias[:, np.clip(dist, -REL_CLIP, REL_CLIP) + REL_CLIP]
    qc = q_pos[:, None] // CHUNK
    kc = k_pos[None, :] // CHUNK
    vis = (k_pos[None, :] >= 0) & (kc <= qc) & (kc >= qc - BAND_CHUNKS)
    return jnp.where(vis[None], _toeplitz(line, n_q, n_k), NEG)


def _band_heads(q_ref, k_ref, v_ref, bm_ref, o_ref, k_rows):
    tq = q_ref.shape[0]
    for p in range(q_ref.shape[-1] // LANES):
        lanes = slice(p * LANES, (p + 1) * LANES)
        qs = jnp.concatenate(_head_split(q_ref[:, lanes], 2), axis=0)
        bias = bm_ref[2 * p:2 * p + 2].reshape(2 * tq, -1)
        o = _softmax_pv(_dot_nt(qs, k_ref[k_rows, lanes]) + bias, v_ref[k_rows, lanes])
        o_ref[:, lanes] = _head_merge(o[:tq], o[tq:]).astype(BF16)


def _band_prompt_kernel(q_ref, k_ref, v_ref, bm_ref, o_ref):
    i = pl.program_id(1)
    start = pl.multiple_of(jnp.maximum(i * BAND_Q_BLOCK - WINDOW_B, 0), BAND_Q_BLOCK)
    _band_heads(q_ref, k_ref, v_ref, bm_ref, o_ref, pl.ds(start, BAND_WIN))


def _band_prompt_call(q, k, v, rel_bias):
    b, t, _ = q.shape
    tq = BAND_Q_BLOCK
    assert t % tq == 0 and t >= BAND_WIN
    bm = jnp.stack([_band_bias(rel_bias, e * tq, tq, max(e * tq - WINDOW_B, 0), BAND_WIN)
                    for e in range(BAND_EDGE + 1)])
    q_spec = pl.BlockSpec((None, tq, TOK_W), lambda bi, i: (bi, i, 0))
    kv_spec = pl.BlockSpec((None, t, TOK_W), lambda bi, i: (bi, 0, 0))
    bm_spec = pl.BlockSpec((None, N_HEADS_TOK, tq, BAND_WIN), lambda bi, i: (jnp.minimum(i, BAND_EDGE), 0, 0, 0))
    return pl.pallas_call(
        _band_prompt_kernel,
        grid=(b, t // tq),
        in_specs=[q_spec, kv_spec, kv_spec, bm_spec],
        out_specs=q_spec,
        out_shape=jax.ShapeDtypeStruct((b, t, TOK_W), BF16),
        compiler_params=pltpu.CompilerParams(dimension_semantics=("arbitrary",) * 2,
                                             vmem_limit_bytes=VMEM_LIMIT),
        name="band_prompt",
    )(q, k, v, bm)


def _band_step_kernel(q_ref, k_ref, v_ref, bm_ref, o_ref):
    _band_heads(q_ref, k_ref, v_ref, bm_ref, o_ref, slice(None))


def _band_step_call(q, k_all, v_all, rel_bias, q_start):
    b, t, _ = q.shape
    tk = k_all.shape[1]
    bm = _band_bias(rel_bias, q_start, t, q_start + t - tk, tk)
    q_spec = pl.BlockSpec((None, t, TOK_W), lambda bi: (bi, 0, 0))
    kv_spec = pl.BlockSpec((None, tk, TOK_W), lambda bi: (bi, 0, 0))
    return pl.pallas_call(
        _band_step_kernel,
        grid=(b,),
        in_specs=[q_spec, kv_spec, kv_spec, _const_spec(bm.shape)],
        out_specs=q_spec,
        out_shape=jax.ShapeDtypeStruct((b, t, TOK_W), BF16),
        compiler_params=pltpu.CompilerParams(dimension_semantics=("arbitrary",)),
        name="band_step",
    )(q, k_all, v_all, bm)


def _split_cols(w, widths):
    out, c = [], 0
    for wd in widths:
        out.append(w[:, c:c + wd].astype(BF16))
        c += wd
    return out


def _prep_layers(params):
    depth = params["g_ff1"].shape[0]
    n_a = params["w_in_a"].shape[0]
    layers = []
    for l in range(depth):
        lay = {name: params[name][l].reshape(1, D_MODEL) for name in ("g_ff1", "g_mix", "g_ff2")}
        lay["ff1"] = (params["w_ff1_gu"][l].astype(BF16), params["w_ff1_down"][l].astype(BF16))
        lay["ff2"] = (params["w_ff2_gu"][l].astype(BF16), params["w_ff2_down"][l].astype(BF16))
        if l < n_a:
            lay["w_in"] = _split_cols(params["w_in_a"][l], (TOK_W, TOK_W, TOK_W, MEM_W))
        else:
            lay["w_in"] = _split_cols(params["w_in_b"][l - n_a], (TOK_W, MEM_W))
            lay["rel_bias"] = params["rel_bias_b"][l - n_a]
        lay["w_out"] = (params["w_out"][l][:TOK_W].astype(BF16), params["w_out"][l][TOK_W:].astype(BF16))
        if l == n_a - 1:
            lay["tail"] = "kv"
            lay["tail_args"] = [params["g_kv"].reshape(1, D_MODEL),
                                *_split_cols(params["w_kv_b"], (TOK_W, TOK_W))]
        elif l == depth - 1:
            lay["tail"], lay["tail_args"] = "final", [params["g_final"].reshape(1, D_MODEL)]
        else:
            lay["tail"], lay["tail_args"] = "none", []
        layers.append(lay)
    return layers, n_a


def _run(x, mem_k, mem_v, past_a_k, past_a_v, past_b_k, past_b_v, layers, n_a):
    bn, t, _ = x.shape
    n = bn * t
    has_past = past_a_k is not None
    q_start = past_a_k.shape[2] if has_past else 0
    x = x.reshape(n, D_MODEL)
    k_stack = v_stack = kb_tail = vb_tail = kb16 = vb16 = None
    for l, lay in enumerate(layers):
        pre_args = (lay["g_ff1"], *lay["ff1"], lay["g_mix"], lay["w_in"], mem_k[l], mem_v[l])
        if l < n_a:
            x, q, k_stack, v_stack, o_mem = _pre_call(x, t, *pre_args, kv_slot=(l, n_a, k_stack, v_stack))
            q = q.reshape(bn, t, TOK_W)
            if has_past:
                pad = jnp.zeros((bn, (-(q_start + t)) % SB_K_BLOCK, TOK_W), F32)
                k = jnp.concatenate([past_a_k[l].reshape(bn, q_start, TOK_W), k_stack[l].reshape(bn, t, TOK_W), pad], 1)
                v = jnp.concatenate([past_a_v[l].reshape(bn, q_start, TOK_W), v_stack[l].reshape(bn, t, TOK_W), pad], 1)
                o_tok = _sb_call(q, k, v, q_start)
            else:
                o_tok = _sb_call(q, k_stack.reshape(n_a, bn, t, TOK_W), v_stack.reshape(n_a, bn, t, TOK_W),
                                 q_start, layer=l)
        else:
            x, q, o_mem = _pre_call(x, t, *pre_args)
            q = q.reshape(bn, t, TOK_W)
            k16 = kb16.reshape(bn, t, TOK_W)
            v16 = vb16.reshape(bn, t, TOK_W)
            if has_past:
                k_all = jnp.concatenate([past_b_k.reshape(bn, -1, TOK_W).astype(BF16), k16], axis=1)
                v_all = jnp.concatenate([past_b_v.reshape(bn, -1, TOK_W).astype(BF16), v16], axis=1)
                o_tok = _band_step_call(q, k_all, v_all, lay["rel_bias"], q_start)
            else:
                o_tok = _band_prompt_call(q, k16, v16, lay["rel_bias"])
        outs = _post_call(x, t, o_tok.reshape(n, TOK_W), o_mem, *lay["w_out"], lay["g_ff2"], *lay["ff2"],
                          lay["tail"], lay["tail_args"])
        x = outs[0]
        if lay["tail"] == "kv":
            _, kb_tail, vb_tail, kb16, vb16 = outs
    heads = lambda a, lead: a.reshape(lead + (-1, N_HEADS_TOK, HEAD_DIM))
    return (x.reshape(bn, t, D_MODEL), heads(k_stack, (n_a, bn)), heads(v_stack, (n_a, bn)),
            heads(kb_tail, (bn,)), heads(vb_tail, (bn,)))


def kernel(x_prompt, x_sample, cache_a_k, cache_a_v, cache_b_k, cache_b_v, cache_mem_k, cache_mem_v,
           mem_prompt, g_ff1, w_ff1_gu, w_ff1_down, g_mix, w_in_a, w_in_b, w_out, g_mem, w_mem_kv,
           g_kv, w_kv_b, rel_bias_b, g_ff2, w_ff2_gu, w_ff2_down, g_final):
    params = dict(g_ff1=g_ff1, w_ff1_gu=w_ff1_gu, w_ff1_down=w_ff1_down, g_mix=g_mix, w_in_a=w_in_a,
                  w_in_b=w_in_b, w_out=w_out, g_kv=g_kv, w_kv_b=w_kv_b, rel_bias_b=rel_bias_b,
                  g_ff2=g_ff2, w_ff2_gu=w_ff2_gu, w_ff2_down=w_ff2_down, g_final=g_final)
    depth = g_mem.shape[0]
    bn, n_mem, _ = mem_prompt.shape

    mk, mv = _memkv_call(mem_prompt.reshape(bn * n_mem, D_MODEL), g_mem.reshape(depth, 1, D_MODEL),
                         w_mem_kv[:, :, :MEM_W].astype(BF16), w_mem_kv[:, :, MEM_W:].astype(BF16))
    mk = mk.reshape(depth, bn, n_mem, MEM_W)
    mv = mv.reshape(depth, bn, n_mem, MEM_W)
    mem_heads = lambda a: a.reshape(a.shape[:-1] + (N_HEADS_MEM, HEAD_DIM))

    layers, n_a = _prep_layers(params)
    y_prompt, a_k_prompt, a_v_prompt, b_k_prompt, b_v_prompt = _run(
        x_prompt, mk, mv, None, None, None, None, layers, n_a)

    dn = cache_mem_k.shape[1]
    y_sample, a_k_sample, a_v_sample, b_k_sample, b_v_sample = _run(
        x_sample, cache_mem_k.reshape(depth, dn, n_mem, MEM_W), cache_mem_v.reshape(depth, dn, n_mem, MEM_W),
        cache_a_k, cache_a_v, cache_b_k, cache_b_v, layers, n_a)

    return (y_prompt, y_sample, a_k_prompt, a_v_prompt, b_k_prompt, b_v_prompt, mem_heads(mk), mem_heads(mv),
            a_k_sample, a_v_sample, b_k_sample, b_v_sample)
```

```python
import functools

import numpy as np
import jax
import jax.numpy as jnp
from jax import lax
from jax.experimental import pallas as pl
from jax.experimental.pallas import tpu as pltpu

F32 = jnp.float32
BF16 = jnp.bfloat16

D_MODEL = 1024
HEAD_DIM = 64
CHUNK = 64
N_MEM = 256
N_HEADS_MEM = 4
N_HEADS_TOK = 12
TOK_W = N_HEADS_TOK * HEAD_DIM
MEM_W = N_HEADS_MEM * HEAD_DIM
BAND_CHUNKS = 8
WINDOW_B = BAND_CHUNKS * CHUNK
REL_CLIP = 128
EPS = 1e-6
NEG = -1e30
QK_SCALE = HEAD_DIM ** -0.5
LOG2E = 1.4426950408889634

LANES = 128
N_PAIRS_TOK = TOK_W // LANES
N_PAIRS_MEM = MEM_W // LANES
FF_CHUNK = 256
TOKEN_TILE = 512
SB_Q_BLOCK = 256
SB_K_BLOCK = 256
SB_LANES = 2 * LANES
SB_EXIT = 160.0
BAND_Q_BLOCK = 2 * CHUNK
BAND_WIN = WINDOW_B + BAND_Q_BLOCK
BAND_EDGE = WINDOW_B // BAND_Q_BLOCK
VMEM_LIMIT = 56 * 1024 * 1024


def _rms(x, g):
    return x * lax.rsqrt(jnp.mean(x * x, axis=-1, keepdims=True) + EPS) * g


def _dot(a, b):
    return jnp.dot(a, b, preferred_element_type=F32)


def _dot_nt(a, b):
    return lax.dot_general(a, b, (((1,), (1,)), ((), ())), preferred_element_type=F32)


def _neg_abs(x):
    return pltpu.bitcast(pltpu.bitcast(x, jnp.uint32) | jnp.uint32(0x80000000), F32)


def _head_split(x, n_heads):
    head = jnp.right_shift(lax.broadcasted_iota(jnp.int32, x.shape, 1), HEAD_DIM.bit_length() - 1)
    zero = jnp.zeros_like(x)
    return [jnp.where(head == h, x, zero) for h in range(n_heads)]


def _head_merge(o0, o1):
    lane = lax.broadcasted_iota(jnp.int32, o0.shape, 1)
    return jnp.where(lane < HEAD_DIM, o0, o1)


def _swiglu_residual(x_ref, g_ref, wgu_ref, wd_ref, h_ref, a_ref):
    h_ref[...] = _rms(x_ref[...], g_ref[...]).astype(BF16)
    f = wd_ref.shape[0]
    for lo in range(0, f, FF_CHUNK):
        gate = _dot(h_ref[...], wgu_ref[:, lo:lo + FF_CHUNK])
        up = _dot(h_ref[...], wgu_ref[:, f + lo:f + lo + FF_CHUNK])
        a_ref[:, lo:lo + FF_CHUNK] = (gate * (1.0 / (1.0 + jnp.exp(-gate))) * up).astype(BF16)
    return x_ref[...] + 0.5 * _dot(a_ref[...], wd_ref[...])


def _softmax_pv(s, v):
    m = jnp.max(s, axis=-1, keepdims=True)
    e = jnp.exp(s - m)
    l = jnp.sum(e, axis=-1, keepdims=True)
    return _dot(e.astype(BF16), v) / l


def _pre_kernel(*refs, has_kv, n_alias, n_streams):
    if has_kv:
        (x_ref, gff_ref, wgu_ref, wd_ref, gmix_ref, wq_ref, wk_ref, wv_ref, wqm_ref, mk_ref, mv_ref) = refs[:11]
        xo_ref, q_ref, k_ref, v_ref, om_ref, h_ref, a_ref = refs[11 + n_alias:]
    else:
        (x_ref, gff_ref, wgu_ref, wd_ref, gmix_ref, wq_ref, wqm_ref, mk_ref, mv_ref,
         xo_ref, q_ref, om_ref, h_ref, a_ref) = refs
    xo_ref[...] = _swiglu_residual(x_ref, gff_ref, wgu_ref, wd_ref, h_ref, a_ref)
    h_ref[...] = _rms(xo_ref[...], gmix_ref[...]).astype(BF16)
    q_ref[...] = (_dot(h_ref[...], wq_ref[...]) * QK_SCALE).astype(BF16)
    if has_kv:
        k_ref[...] = _dot_nt(wk_ref[...], h_ref[...])
        v_ref[...] = _dot_nt(wv_ref[...], h_ref[...])
    qm =(_dot(h_ref[...], wqm_ref[...]) * QK_SCALE).astype(BF16)
    rows = x_ref.shape[0] // n_streams
    for s in range(n_streams):
        r0 = s * rows
        for p in range(N_PAIRS_MEM):
            c0 = p * LANES
            mk = mk_ref[s, :, c0:c0 + LANES].astype(BF16)
            mv = mv_ref[s, :, c0:c0 + LANES].astype(BF16)
            qa, qb = _head_split(qm[r0:r0 + rows, c0:c0 + LANES], 2)
            oa = _softmax_pv(_dot_nt(qa, mk), mv)
            ob = _softmax_pv(_dot_nt(qb, mk), mv)
            om_ref[r0:r0 + rows, c0:c0 + LANES] = _head_merge(oa, ob).astype(BF16)


def _const_spec(shape):
    nd = len(shape)
    return pl.BlockSpec(shape, lambda i: (0,) * nd, pipeline_mode=pl.Buffered(1))


def _token_tile(n):
    tm = min(TOKEN_TILE, n)
    assert n % tm == 0
    return tm


def _feature_major(lead, n, seq_len, tm, width=TOK_W):
    lead_shape = tuple(s for s, _ in lead)
    lead_idx = tuple(ix for _, ix in lead)
    squeezed = (None,) * len(lead)
    if tm <= seq_len:
        tiles_per_seq = seq_len // tm
        return (lead_shape + (n // seq_len, width, seq_len),
                pl.BlockSpec(squeezed + (None, width, tm),
                             lambda i: lead_idx + (i // tiles_per_seq, 0, i % tiles_per_seq)))
    return (lead_shape + (width, n), pl.BlockSpec(squeezed + (width, tm), lambda i: lead_idx + (0, i)))


def _pre_call(x, seq_len, g_ff, wgu, wd, g_mix, w_in_parts, mem_k, mem_v, kv_slot=None):
    n = x.shape[0]
    tm = _token_tile(n)
    has_kv = kv_slot is not None
    if tm >= seq_len:
        assert tm % seq_len == 0
        n_streams = tm // seq_len
        mem_map = lambda i: (i, 0, 0)
    else:
        assert seq_len % tm == 0
        n_streams = 1
        tiles_per_seq = seq_len // tm
        mem_map = lambda i: (i // tiles_per_seq, 0, 0)
    row = lambda w: pl.BlockSpec((tm, w), lambda i: (i, 0))
    mem_spec = pl.BlockSpec((n_streams, N_MEM, MEM_W), mem_map)
    args = [x, g_ff, wgu, wd, g_mix, *w_in_parts, mem_k, mem_v]
    in_specs = [row(D_MODEL)] + [_const_spec(a.shape) for a in args[1:-2]] + [mem_spec, mem_spec]
    out_shape = [jax.ShapeDtypeStruct((n, D_MODEL), F32), jax.ShapeDtypeStruct((n, TOK_W), BF16)]
    out_specs = [row(D_MODEL), row(TOK_W)]
    aliases = {}
    n_alias = 0
    if has_kv:
        layer, n_layers, k_stack, v_stack = kv_slot
        kv_shape, kv_spec = _feature_major(((n_layers, layer),), n, seq_len, tm)
        out_shape += [jax.ShapeDtypeStruct(kv_shape, F32)] * 2
        out_specs += [kv_spec] * 2
        if k_stack is not None:
            n_alias = 2
            aliases = {len(args): 2, len(args) + 1: 3}
            args += [k_stack, v_stack]
            in_specs += [pl.BlockSpec(memory_space=pl.ANY)] * 2
    out_shape.append(jax.ShapeDtypeStruct((n, MEM_W), BF16))
    out_specs.append(row(MEM_W))
    d_ff = wd.shape[0]
    return pl.pallas_call(
        functools.partial(_pre_kernel, has_kv=has_kv, n_alias=n_alias, n_streams=n_streams),
        grid=(n // tm,),
        in_specs=in_specs,
        out_specs=out_specs,
        out_shape=out_shape,
        input_output_aliases=aliases,
        scratch_shapes=[pltpu.VMEM((tm, D_MODEL), BF16), pltpu.VMEM((tm, d_ff), BF16)],
        compiler_params=pltpu.CompilerParams(dimension_semantics=("arbitrary",),
                                             vmem_limit_bytes=VMEM_LIMIT),
        name="pre_kv" if has_kv else "pre_q",
    )(*args)


def _post_kernel(*refs, tail):
    if tail == "kv":
        (x_ref, ot_ref, om_ref, wot_ref, wom_ref, gff_ref, wgu_ref, wd_ref, gt_ref, wkb_ref, wvb_ref,
         xo_ref, kt_ref, vt_ref, k16_ref, v16_ref, h_ref, a_ref) = refs
    elif tail == "final":
        (x_ref, ot_ref, om_ref, wot_ref, wom_ref, gff_ref, wgu_ref, wd_ref, gt_ref,
         xo_ref, h_ref, a_ref) = refs
    else:
        (x_ref, ot_ref, om_ref, wot_ref, wom_ref, gff_ref, wgu_ref, wd_ref,
         xo_ref, h_ref, a_ref) = refs
    xo_ref[...] = x_ref[...] + _dot(ot_ref[...], wot_ref[...]) + _dot(om_ref[...], wom_ref[...])
    xo_ref[...] = _swiglu_residual(xo_ref, gff_ref, wgu_ref, wd_ref, h_ref, a_ref)
    if tail == "final":
        xo_ref[...] = _rms(xo_ref[...], gt_ref[...])
    if tail == "kv":
        h_ref[...] = _rms(xo_ref[...], gt_ref[...]).astype(BF16)
        kb = _dot(h_ref[...], wkb_ref[...])
        vb = _dot(h_ref[...], wvb_ref[...])
        kt_ref[...] = kb
        vt_ref[...] = vb
        k16_ref[...] = kb.astype(BF16)
        v16_ref[...] = vb.astype(BF16)


def _post_call(x, seq_len, o_tok, o_mem, w_out_tok, w_out_mem, g_ff, wgu, wd, tail, tail_args):
    n = x.shape[0]
    tm = _token_tile(n)
    row = lambda w: pl.BlockSpec((tm, w), lambda i: (i, 0))
    consts = [w_out_tok, w_out_mem, g_ff, wgu, wd, *tail_args]
    in_specs = [row(D_MODEL), row(TOK_W), row(MEM_W)] + [_const_spec(c.shape) for c in consts]
    out_shape = [jax.ShapeDtypeStruct((n, D_MODEL), F32)]
    out_specs = [row(D_MODEL)]
    if tail == "kv":
        keep = min(WINDOW_B, seq_len)
        if keep == seq_len:
            tail_rows, tail_spec = n, row(TOK_W)
        else:
            assert keep == tm and seq_len % tm == 0
            tiles_per_seq = seq_len // tm
            tail_rows = (n // seq_len) * keep
            tail_spec = pl.BlockSpec((tm, TOK_W), lambda i: (i // tiles_per_seq, 0))
        out_shape += [jax.ShapeDtypeStruct((tail_rows, TOK_W), F32)] * 2
        out_shape += [jax.ShapeDtypeStruct((n, TOK_W), BF16)] * 2
        out_specs += [tail_spec, tail_spec, row(TOK_W), row(TOK_W)]
    d_ff = wd.shape[0]
    return pl.pallas_call(
        functools.partial(_post_kernel, tail=tail),
        grid=(n // tm,),
        in_specs=in_specs,
        out_specs=out_specs,
        out_shape=out_shape,
        scratch_shapes=[pltpu.VMEM((tm, D_MODEL), BF16), pltpu.VMEM((tm, d_ff), BF16)],
        compiler_params=pltpu.CompilerParams(dimension_semantics=("arbitrary",),
                                             vmem_limit_bytes=VMEM_LIMIT),
        name="post_" + tail,
    )(x, o_tok, o_mem, *consts)


def _memkv_kernel(m_ref, g_ref, wk_ref, wv_ref, k_ref, v_ref):
    h = _rms(m_ref[...], g_ref[0]).astype(BF16)
    k_ref[0] = _dot(h, wk_ref[0])
    v_ref[0] = _dot(h, wv_ref[0])


def _memkv_call(mem, g_mem, wk, wv):
    n = mem.shape[0]
    depth = g_mem.shape[0]
    tm = _token_tile(n)
    w_spec = pl.BlockSpec((1, D_MODEL, MEM_W), lambda l, i: (l, 0, 0))
    o_spec = pl.BlockSpec((1, tm, MEM_W), lambda l, i: (l, i, 0))
    return pl.pallas_call(
        _memkv_kernel,
        grid=(depth, n // tm),
        in_specs=[pl.BlockSpec((tm, D_MODEL), lambda l, i: (i, 0)),
                  pl.BlockSpec((1, 1, D_MODEL), lambda l, i: (l, 0, 0)), w_spec, w_spec],
        out_specs=[o_spec, o_spec],
        out_shape=[jax.ShapeDtypeStruct((depth, n, MEM_W), F32)] * 2,
        compiler_params=pltpu.CompilerParams(dimension_semantics=("arbitrary", "arbitrary")),
        name="mem_kv",
    )(mem, g_mem, wk, wv)


def _sb_kernel(q_ref, k_ref, v_ref, o_ref, kb_ref, vb_ref, qs_ref, acc_ref, *, q_start):
    i = pl.program_id(2)
    tq, width = q_ref.shape
    tk = SB_K_BLOCK
    n_heads = width // HEAD_DIM

    @pl.when(i == 0)
    def _():
        for blk in range(kb_ref.shape[0]):
            kb_ref[blk] = k_ref[:, blk * tk:(blk + 1) * tk].astype(BF16)
            vb_ref[blk] = v_ref[:, blk * tk:(blk + 1) * tk].astype(BF16)

    for h, qh in enumerate(_head_split(q_ref[...], n_heads)):
        qs_ref[h * tq:(h + 1) * tq, :] = qh
    rows = n_heads * tq
    q0 = q_start + i * tq
    u_after = jnp.where(lax.broadcasted_iota(jnp.int32, (tk, tk), 0)
                        > lax.broadcasted_iota(jnp.int32, (tk, tk), 1), 1.0, 0.0).astype(BF16)
    n_blocks = jnp.minimum((q0 + tq - 2) // tk + 1, kb_ref.shape[0])

    def sweep_block(t, run, newest):
        blk = n_blocks - 1 - t
        ks = blk * tk
        z = _dot(qs_ref[...], kb_ref[blk]) * LOG2E
        sp = jnp.maximum(z, 0.0) + jnp.log2(1.0 + jnp.exp2(_neg_abs(z)))
        if newest:
            q_pos = q0 + (lax.broadcasted_iota(jnp.int32, (rows, tk), 0) & (tq - 1))
            vis = ks + lax.broadcasted_iota(jnp.int32, (rows, tk), 1) < q_pos
            sp = jnp.where(vis, sp, 0.0)
        sp16 = sp.astype(BF16)
        after = _dot(sp16, u_after)
        if newest:
            w = jnp.where(vis, jnp.exp2((z - sp) - after), 0.0)
        else:
            w = jnp.exp2((z - sp) - after - run)
        w = w.astype(BF16)
        for p in range(n_heads // 2):
            pair_rows = slice(2 * p * tq, (2 * p + 2) * tq)
            pv = _dot_nt(w[pair_rows], vb_ref[blk, p * LANES:(p + 1) * LANES, :])
            if newest:
                acc_ref[pair_rows] = pv
            else:
                acc_ref[pair_rows] += pv
        return run + after[:, 0:1] + sp16[:, 0:1].astype(F32)

    def unfinished(run):
        return (jnp.min(run) < SB_EXIT).astype(jnp.int32)

    def cond(carry):
        t, go, _ = carry
        return jnp.logical_and(t < n_blocks, go > 0)

    def body(carry):
        t, _, run = carry
        run = sweep_block(t, run, False)
        return t + 1, unfinished(run), run

    run = sweep_block(0, jnp.zeros((rows, 1), F32), True)
    lax.while_loop(cond, body, (jnp.int32(1), unfinished(run), run))
    for p in range(n_heads // 2):
        r0 = 2 * p * tq
        o_ref[:, p * LANES:(p + 1) * LANES] = _head_merge(
            acc_ref[r0:r0 + tq], acc_ref[r0 + tq:r0 + 2 * tq]).astype(BF16)


def _sb_call(q, k, v, q_start, layer=None):
    b, t, _ = q.shape
    tk_all = k.shape[-1]
    tq = min(SB_Q_BLOCK, t)
    assert t % tq == 0 and SB_K_BLOCK % tq == 0 and q_start % tq == 0 and tk_all % SB_K_BLOCK == 0
    assert tq & (tq - 1) == 0
    q_spec = pl.BlockSpec((None, tq, SB_LANES), lambda bi, g, i: (bi, i, g))
    if layer is None:
        kv_spec = pl.BlockSpec((None, SB_LANES, tk_all), lambda bi, g, i: (bi, g, 0))
    else:
        kv_spec = pl.BlockSpec((None, None, SB_LANES, tk_all), lambda bi, g, i: (layer, bi, g, 0))
    kv_scratch = pltpu.VMEM((tk_all // SB_K_BLOCK, SB_LANES, SB_K_BLOCK), BF16)
    return pl.pallas_call(
        functools.partial(_sb_kernel, q_start=q_start),
        grid=(b, TOK_W // SB_LANES, t // tq),
        in_specs=[q_spec, kv_spec, kv_spec],
        out_specs=q_spec,
        out_shape=jax.ShapeDtypeStruct((b, t, TOK_W), BF16),
        scratch_shapes=[kv_scratch, kv_scratch,
                        pltpu.VMEM((SB_LANES // HEAD_DIM * tq, SB_LANES), BF16),
                        pltpu.VMEM((SB_LANES // HEAD_DIM * tq, LANES), F32)],
        compiler_params=pltpu.CompilerParams(dimension_semantics=("arbitrary",) * 3),
        name="stick_breaking",
    )(q, k, v)


def _toeplitz(line, rows, cols):
    period = rows + cols
    e = jnp.roll(jnp.pad(line, ((0, 0), (0, 1))), -(rows - 1), axis=1)
    tiled = jnp.tile(e, (1, rows))[:, :rows * (period - 1)]
    return tiled.reshape(line.shape[0], rows, period - 1)[:, :, :cols]


def _band_bias(rel_bias, q_pos0, n_q, k_pos0, n_k):
    q_pos = q_pos0 + np.arange(n_q)
    k_pos = k_pos0 + np.arange(n_k)
    dist = (q_pos0 - k_pos0) + (n_q - 1) - np.arange(n_q + n_k - 1)
    line = rel_bias[:, np.clip(dist, -REL_CLIP, REL_CLIP) + REL_CLIP]
    qc = q_pos[:, None] // CHUNK
    kc = k_pos[None, :] // CHUNK
    vis = (k_pos[None, :] >= 0) & (kc <= qc) & (kc >= qc - BAND_CHUNKS)
    return jnp.where(vis[None], _toeplitz(line, n_q, n_k), NEG)


def _band_heads(q_ref, k_ref, v_ref, bm_ref, o_ref, k_rows):
    tq = q_ref.shape[0]
    for p in range(q_ref.shape[-1] // LANES):
        lanes = slice(p * LANES, (p + 1) * LANES)
        qs = jnp.concatenate(_head_split(q_ref[:, lanes], 2), axis=0)
        bias = bm_ref[2 * p:2 * p + 2].reshape(2 * tq, -1)
        o = _softmax_pv(_dot_nt(qs, k_ref[k_rows, lanes]) + bias, v_ref[k_rows, lanes])
        o_ref[:, lanes] = _head_merge(o[:tq], o[tq:]).astype(BF16)


def _band_prompt_kernel(q_ref, k_ref, v_ref, bm_ref, o_ref):
    i = pl.program_id(1)
    start = pl.multiple_of(jnp.maximum(i * BAND_Q_BLOCK - WINDOW_B, 0), BAND_Q_BLOCK)
    _band_heads(q_ref, k_ref, v_ref, bm_ref, o_ref, pl.ds(start, BAND_WIN))


def _band_prompt_call(q, k, v, rel_bias):
    b, t, _ = q.shape
    tq = BAND_Q_BLOCK
    assert t % tq == 0 and t >= BAND_WIN
    bm = jnp.stack([_band_bias(rel_bias, e * tq, tq, max(e * tq - WINDOW_B, 0), BAND_WIN)
                    for e in range(BAND_EDGE + 1)])
    q_spec = pl.BlockSpec((None, tq, TOK_W), lambda bi, i: (bi, i, 0))
    kv_spec = pl.BlockSpec((None, t, TOK_W), lambda bi, i: (bi, 0, 0))
    bm_spec = pl.BlockSpec((None, N_HEADS_TOK, tq, BAND_WIN), lambda bi, i: (jnp.minimum(i, BAND_EDGE), 0, 0, 0))
    return pl.pallas_call(
        _band_prompt_kernel,
        grid=(b, t // tq),
        in_specs=[q_spec, kv_spec, kv_spec, bm_spec],
        out_specs=q_spec,
        out_shape=jax.ShapeDtypeStruct((b, t, TOK_W), BF16),
        compiler_params=pltpu.CompilerParams(dimension_semantics=("arbitrary",) * 2,
                                             vmem_limit_bytes=VMEM_LIMIT),
        name="band_prompt",
    )(q, k, v, bm)


def _band_step_kernel(q_ref, k_ref, v_ref, bm_ref, o_ref):
    _band_heads(q_ref, k_ref, v_ref, bm_ref, o_ref, slice(None))


def _band_step_call(q, k_all, v_all, rel_bias, q_start):
    b, t, _ = q.shape
    tk = k_all.shape[1]
    bm = _band_bias(rel_bias, q_start, t, q_start + t - tk, tk)
    q_spec = pl.BlockSpec((None, t, TOK_W), lambda bi: (bi, 0, 0))
    kv_spec = pl.BlockSpec((None, tk, TOK_W), lambda bi: (bi, 0, 0))
    return pl.pallas_call(
        _band_step_kernel,
        grid=(b,),
        in_specs=[q_spec, kv_spec, kv_spec, _const_spec(bm.shape)],
        out_specs=q_spec,
        out_shape=jax.ShapeDtypeStruct((b, t, TOK_W), BF16),
        compiler_params=pltpu.CompilerParams(dimension_semantics=("arbitrary",)),
        name="band_step",
    )(q, k_all, v_all, bm)


def _split_cols(w, widths):
    out, c = [], 0
    for wd in widths:
        out.append(w[:, c:c + wd].astype(BF16))
        c += wd
    return out


def _prep_layers(params):
    depth = params["g_ff1"].shape[0]
    n_a = params["w_in_a"].shape[0]
    layers = []
    for l in range(depth):
        lay = {name: params[name][l].reshape(1, D_MODEL) for name in ("g_ff1", "g_mix", "g_ff2")}
        lay["ff1"] = (params["w_ff1_gu"][l].astype(BF16), params["w_ff1_down"][l].astype(BF16))
        lay["ff2"] = (params["w_ff2_gu"][l].astype(BF16), params["w_ff2_down"][l].astype(BF16))
        if l < n_a:
            wq, wk, wv, wqm = _split_cols(params["w_in_a"][l], (TOK_W, TOK_W, TOK_W, MEM_W))
            lay["w_in"] = [wq, wk.T, wv.T, wqm]
        else:
            lay["w_in"] = _split_cols(params["w_in_b"][l - n_a], (TOK_W, MEM_W))
            lay["rel_bias"] = params["rel_bias_b"][l - n_a]
        lay["w_out"] = (params["w_out"][l][:TOK_W].astype(BF16), params["w_out"][l][TOK_W:].astype(BF16))
        if l == n_a - 1:
            lay["tail"] = "kv"
            lay["tail_args"] = [params["g_kv"].reshape(1, D_MODEL),
                                *_split_cols(params["w_kv_b"], (TOK_W, TOK_W))]
        elif l == depth - 1:
            lay["tail"], lay["tail_args"] = "final", [params["g_final"].reshape(1, D_MODEL)]
        else:
            lay["tail"], lay["tail_args"] = "none", []
        layers.append(lay)
    return layers, n_a


def _per_stream(a, bn, t):
    if a.shape[-1] == t:
        return a
    return jnp.moveaxis(a.reshape(a.shape[:-1] + (bn, t)), -2, -3)


def _rows_to_features(a):
    return a.reshape(a.shape[0], a.shape[1], TOK_W).transpose(0, 2, 1)


def _features_to_heads(a):
    a = a.reshape(a.shape[:-2] + (N_HEADS_TOK, HEAD_DIM, a.shape[-1]))
    return jnp.moveaxis(a, -1, -3)


def _run(x, mem_k, mem_v, past_a_k, past_a_v, past_b_k, past_b_v, layers, n_a):
    bn, t, _ = x.shape
    n = bn * t
    has_past = past_a_k is not None
    q_start = past_a_k.shape[2] if has_past else 0
    x = x.reshape(n, D_MODEL)
    k_stack = v_stack = kb_tail = vb_tail = kb16 = vb16 = None
    for l, lay in enumerate(layers):
        pre_args = (lay["g_ff1"], *lay["ff1"], lay["g_mix"], lay["w_in"], mem_k[l], mem_v[l])
        if l < n_a:
            x, q, k_stack, v_stack, o_mem = _pre_call(x, t, *pre_args, kv_slot=(l, n_a, k_stack, v_stack))
            q = q.reshape(bn, t, TOK_W)
            if has_past:
                pad = jnp.zeros((bn, TOK_W, (-(q_start + t)) % SB_K_BLOCK), F32)
                k = jnp.concatenate([_rows_to_features(past_a_k[l]), _per_stream(k_stack[l], bn, t), pad], 2)
                v = jnp.concatenate([_rows_to_features(past_a_v[l]), _per_stream(v_stack[l], bn, t), pad], 2)
                o_tok = _sb_call(q, k, v, q_start)
            else:
                o_tok = _sb_call(q, k_stack, v_stack, q_start, layer=l)
        else:
            x, q, o_mem = _pre_call(x, t, *pre_args)
            q = q.reshape(bn, t, TOK_W)
            k16 = kb16.reshape(bn, t, TOK_W)
            v16 = vb16.reshape(bn, t, TOK_W)
            if has_past:
                k_all = jnp.concatenate([past_b_k.reshape(bn, -1, TOK_W).astype(BF16), k16], axis=1)
                v_all = jnp.concatenate([past_b_v.reshape(bn, -1, TOK_W).astype(BF16), v16], axis=1)
                o_tok = _band_step_call(q, k_all, v_all, lay["rel_bias"], q_start)
            else:
                o_tok = _band_prompt_call(q, k16, v16, lay["rel_bias"])
        outs = _post_call(x, t, o_tok.reshape(n, TOK_W), o_mem, *lay["w_out"], lay["g_ff2"], *lay["ff2"],
                          lay["tail"], lay["tail_args"])
        x = outs[0]
        if lay["tail"] == "kv":
            _, kb_tail, vb_tail, kb16, vb16 = outs
    return (x.reshape(bn, t, D_MODEL),
            _features_to_heads(_per_stream(k_stack, bn, t)), _features_to_heads(_per_stream(v_stack, bn, t)),
            kb_tail.reshape(bn, -1, N_HEADS_TOK, HEAD_DIM), vb_tail.reshape(bn, -1, N_HEADS_TOK, HEAD_DIM))


def kernel(x_prompt, x_sample, cache_a_k, cache_a_v, cache_b_k, cache_b_v, cache_mem_k, cache_mem_v,
           mem_prompt, g_ff1, w_ff1_gu, w_ff1_down, g_mix, w_in_a, w_in_b, w_out, g_mem, w_mem_kv,
           g_kv, w_kv_b, rel_bias_b, g_ff2, w_ff2_gu, w_ff2_down, g_final):
    params = dict(g_ff1=g_ff1, w_ff1_gu=w_ff1_gu, w_ff1_down=w_ff1_down, g_mix=g_mix, w_in_a=w_in_a,
                  w_in_b=w_in_b, w_out=w_out, g_kv=g_kv, w_kv_b=w_kv_b, rel_bias_b=rel_bias_b,
                  g_ff2=g_ff2, w_ff2_gu=w_ff2_gu, w_ff2_down=w_ff2_down, g_final=g_final)
    depth = g_mem.shape[0]
    bn, n_mem, _ = mem_prompt.shape

    mk, mv = _memkv_call(mem_prompt.reshape(bn * n_mem, D_MODEL), g_mem.reshape(depth, 1, D_MODEL),
                         w_mem_kv[:, :, :MEM_W].astype(BF16), w_mem_kv[:, :, MEM_W:].astype(BF16))
    mk = mk.reshape(depth, bn, n_mem, MEM_W)
    mv = mv.reshape(depth, bn, n_mem, MEM_W)
    mem_heads = lambda a: a.reshape(a.shape[:-1] + (N_HEADS_MEM, HEAD_DIM))

    layers, n_a = _prep_layers(params)
    y_prompt, a_k_prompt, a_v_prompt, b_k_prompt, b_v_prompt = _run(
        x_prompt, mk, mv, None, None, None, None, layers, n_a)

    dn = cache_mem_k.shape[1]
    y_sample, a_k_sample, a_v_sample, b_k_sample, b_v_sample = _run(
        x_sample, cache_mem_k.reshape(depth, dn, n_mem, MEM_W), cache_mem_v.reshape(depth, dn, n_mem, MEM_W),
        cache_a_k, cache_a_v, cache_b_k, cache_b_v, layers, n_a)

    return (y_prompt, y_sample, a_k_prompt, a_v_prompt, b_k_prompt, b_v_prompt, mem_heads(mk), mem_heads(mv),
            a_k_sample, a_v_sample, b_k_sample, b_v_sample)
```

```python
import functools

import numpy as np
import jax
import jax.numpy as jnp
from jax import lax
from jax.experimental import pallas as pl
from jax.experimental.pallas import tpu as pltpu

F32 = jnp.float32
BF16 = jnp.bfloat16

D_MODEL = 1024
HEAD_DIM = 64
CHUNK = 64
N_MEM = 256
N_HEADS_MEM = 4
N_HEADS_TOK = 12
TOK_W = N_HEADS_TOK * HEAD_DIM
MEM_W = N_HEADS_MEM * HEAD_DIM
BAND_CHUNKS = 8
WINDOW_B = BAND_CHUNKS * CHUNK
REL_CLIP = 128
EPS = 1e-6
NEG = -1e30
QK_SCALE = HEAD_DIM ** -0.5
LOG2E = 1.4426950408889634

LANES = 128
N_PAIRS_TOK = TOK_W // LANES
N_PAIRS_MEM = MEM_W // LANES
FF_CHUNK = 256
TOKEN_TILE = 512
SB_Q_BLOCK = 256
SB_K_BLOCK = 256
SB_LANES = 2 * LANES
SB_EXIT = 160.0
BAND_Q_BLOCK = 4 * CHUNK
BAND_WIN = WINDOW_B + BAND_Q_BLOCK
BAND_EDGE = WINDOW_B // BAND_Q_BLOCK
VMEM_LIMIT = 56 * 1024 * 1024


def _rms(x, g):
    return x * lax.rsqrt(jnp.mean(x * x, axis=-1, keepdims=True) + EPS) * g


def _dot(a, b):
    return jnp.dot(a, b, preferred_element_type=F32)


def _dot_nt(a, b):
    return lax.dot_general(a, b, (((1,), (1,)), ((), ())), preferred_element_type=F32)


def _neg_abs(x):
    return pltpu.bitcast(pltpu.bitcast(x, jnp.uint32) | jnp.uint32(0x80000000), F32)


def _head_split(x, n_heads):
    head = jnp.right_shift(lax.broadcasted_iota(jnp.int32, x.shape, 1), HEAD_DIM.bit_length() - 1)
    zero = jnp.zeros_like(x)
    return [jnp.where(head == h, x, zero) for h in range(n_heads)]


def _head_merge(o0, o1):
    lane = lax.broadcasted_iota(jnp.int32, o0.shape, 1)
    return jnp.where(lane < HEAD_DIM, o0, o1)


def _swiglu_residual(x_ref, g_ref, wgu_ref, wd_ref, h_ref, a_ref):
    h_ref[...] = _rms(x_ref[...], g_ref[...]).astype(BF16)
    f = wd_ref.shape[0]
    for lo in range(0, f, FF_CHUNK):
        gate = _dot(h_ref[...], wgu_ref[:, lo:lo + FF_CHUNK])
        up = _dot(h_ref[...], wgu_ref[:, f + lo:f + lo + FF_CHUNK])
        a_ref[:, lo:lo + FF_CHUNK] = (gate * (1.0 / (1.0 + jnp.exp(-gate))) * up).astype(BF16)
    return x_ref[...] + 0.5 * _dot(a_ref[...], wd_ref[...])


def _softmax_pv(s, v, v_feature_major=False):
    m = jnp.max(s, axis=-1, keepdims=True)
    e = jnp.exp(s - m)
    l = jnp.sum(e, axis=-1, keepdims=True)
    pv = _dot_nt(e.astype(BF16), v) if v_feature_major else _dot(e.astype(BF16), v)
    return pv / l


def _pre_kernel(*refs, has_kv, n_alias, n_streams):
    if has_kv:
        (x_ref, gff_ref, wgu_ref, wd_ref, gmix_ref, wq_ref, wk_ref, wv_ref, wqm_ref, mk_ref, mv_ref) = refs[:11]
        xo_ref, q_ref, k_ref, v_ref, om_ref, h_ref, a_ref = refs[11 + n_alias:]
    else:
        (x_ref, gff_ref, wgu_ref, wd_ref, gmix_ref, wq_ref, wqm_ref, mk_ref, mv_ref,
         xo_ref, q_ref, om_ref, h_ref, a_ref) = refs
    xo_ref[...] = _swiglu_residual(x_ref, gff_ref, wgu_ref, wd_ref, h_ref, a_ref)
    h_ref[...] = _rms(xo_ref[...], gmix_ref[...]).astype(BF16)
    q_ref[...] = (_dot(h_ref[...], wq_ref[...]) * QK_SCALE).astype(BF16)
    if has_kv:
        k_ref[...] = _dot_nt(wk_ref[...], h_ref[...])
        v_ref[...] = _dot_nt(wv_ref[...], h_ref[...])
    qm = (_dot(h_ref[...], wqm_ref[...]) * QK_SCALE).astype(BF16)
    rows = x_ref.shape[0] // n_streams
    for s in range(n_streams):
        r0 = s * rows
        for p in range(N_PAIRS_MEM):
            c0 = p * LANES
            mk = mk_ref[s, c0:c0 + LANES, :].astype(BF16)
            mv = mv_ref[s, c0:c0 + LANES, :].astype(BF16)
            qa, qb = _head_split(qm[r0:r0 + rows, c0:c0 + LANES], 2)
            oa = _softmax_pv(_dot(qa, mk), mv, v_feature_major=True)
            ob = _softmax_pv(_dot(qb, mk), mv, v_feature_major=True)
            om_ref[r0:r0 + rows, c0:c0 + LANES] = _head_merge(oa, ob).astype(BF16)


def _const_spec(shape):
    nd = len(shape)
    return pl.BlockSpec(shape, lambda i: (0,) * nd, pipeline_mode=pl.Buffered(1))


def _token_tile(n):
    tm = min(TOKEN_TILE, n)
    assert n % tm == 0
    return tm


def _feature_major(lead, n, seq_len, tm, width=TOK_W):
    lead_shape = tuple(s for s, _ in lead)
    lead_idx = tuple(ix for _, ix in lead)
    squeezed = (None,) * len(lead)
    if tm <= seq_len:
        tiles_per_seq = seq_len // tm
        return (lead_shape + (n // seq_len, width, seq_len),
                pl.BlockSpec(squeezed + (None, width, tm),
                             lambda i: lead_idx + (i // tiles_per_seq, 0, i % tiles_per_seq)))
    return (lead_shape + (width, n), pl.BlockSpec(squeezed + (width, tm), lambda i: lead_idx + (0, i)))


def _pre_call(x, seq_len, g_ff, wgu, wd, g_mix, w_in_parts, mem_k, mem_v, kv_slot=None):
    n = x.shape[0]
    tm = _token_tile(n)
    has_kv = kv_slot is not None
    mem_layer, mem_k = mem_k
    _, mem_v = mem_v
    if tm >= seq_len:
        assert tm % seq_len == 0
        n_streams = tm // seq_len
        mem_map = lambda i: (mem_layer, i, 0, 0)
    else:
        assert seq_len % tm == 0
        n_streams = 1
        tiles_per_seq = seq_len // tm
        mem_map = lambda i: (mem_layer, i // tiles_per_seq, 0, 0)
    row = lambda w: pl.BlockSpec((tm, w), lambda i: (i, 0))
    mem_spec = pl.BlockSpec((None, n_streams, MEM_W, N_MEM), mem_map)
    args = [x, g_ff, wgu, wd, g_mix, *w_in_parts, mem_k, mem_v]
    in_specs = [row(D_MODEL)] + [_const_spec(a.shape) for a in args[1:-2]] + [mem_spec, mem_spec]
    out_shape = [jax.ShapeDtypeStruct((n, D_MODEL), F32), jax.ShapeDtypeStruct((n, TOK_W), BF16)]
    out_specs = [row(D_MODEL), row(TOK_W)]
    aliases = {}
    n_alias = 0
    if has_kv:
        layer, n_layers, k_stack, v_stack = kv_slot
        kv_shape, kv_spec = _feature_major(((n_layers, layer),), n, seq_len, tm)
        out_shape += [jax.ShapeDtypeStruct(kv_shape, F32)] * 2
        out_specs += [kv_spec] * 2
        if k_stack is not None:
            n_alias = 2
            aliases = {len(args): 2, len(args) + 1: 3}
            args += [k_stack, v_stack]
            in_specs += [pl.BlockSpec(memory_space=pl.ANY)] * 2
    out_shape.append(jax.ShapeDtypeStruct((n, MEM_W), BF16))
    out_specs.append(row(MEM_W))
    d_ff = wd.shape[0]
    return pl.pallas_call(
        functools.partial(_pre_kernel, has_kv=has_kv, n_alias=n_alias, n_streams=n_streams),
        grid=(n // tm,),
        in_specs=in_specs,
        out_specs=out_specs,
        out_shape=out_shape,
        input_output_aliases=aliases,
        scratch_shapes=[pltpu.VMEM((tm, D_MODEL), BF16), pltpu.VMEM((tm, d_ff), BF16)],
        compiler_params=pltpu.CompilerParams(dimension_semantics=("arbitrary",),
                                             vmem_limit_bytes=VMEM_LIMIT),
        name="pre_kv" if has_kv else "pre_q",
    )(*args)


def _post_kernel(*refs, tail):
    if tail == "kv":
        (x_ref, ot_ref, om_ref, wot_ref, wom_ref, gff_ref, wgu_ref, wd_ref, gt_ref, wkb_ref, wvb_ref,
         xo_ref, kt_ref, vt_ref, k16_ref, v16_ref, h_ref, a_ref) = refs
    elif tail == "final":
        (x_ref, ot_ref, om_ref, wot_ref, wom_ref, gff_ref, wgu_ref, wd_ref, gt_ref,
         xo_ref, h_ref, a_ref) = refs
    else:
        (x_ref, ot_ref, om_ref, wot_ref, wom_ref, gff_ref, wgu_ref, wd_ref,
         xo_ref, h_ref, a_ref) = refs
    xo_ref[...] = x_ref[...] + _dot(ot_ref[...], wot_ref[...]) + _dot(om_ref[...], wom_ref[...])
    xo_ref[...] = _swiglu_residual(xo_ref, gff_ref, wgu_ref, wd_ref, h_ref, a_ref)
    if tail == "final":
        xo_ref[...] = _rms(xo_ref[...], gt_ref[...])
    if tail == "kv":
        h_ref[...] = _rms(xo_ref[...], gt_ref[...]).astype(BF16)
        kb = _dot_nt(wkb_ref[...], h_ref[...])
        vb = _dot(h_ref[...], wvb_ref[...])
        kt_ref[...] = kb
        vt_ref[...] = vb
        k16_ref[...] = kb.astype(BF16)
        v16_ref[...] = vb.astype(BF16)


def _post_call(x, seq_len, o_tok, o_mem, w_out_tok, w_out_mem, g_ff, wgu, wd, tail, tail_args):
    n = x.shape[0]
    tm = _token_tile(n)
    row = lambda w: pl.BlockSpec((tm, w), lambda i: (i, 0))
    consts = [w_out_tok, w_out_mem, g_ff, wgu, wd, *tail_args]
    in_specs = [row(D_MODEL), row(TOK_W), row(MEM_W)] + [_const_spec(c.shape) for c in consts]
    out_shape = [jax.ShapeDtypeStruct((n, D_MODEL), F32)]
    out_specs = [row(D_MODEL)]
    if tail == "kv":
        keep = min(WINDOW_B, seq_len)
        k16_shape, k16_spec = _feature_major((), n, seq_len, tm)
        if keep == seq_len:
            kt_shape, kt_spec, vt_shape, vt_spec = k16_shape, k16_spec, (n, TOK_W), row(TOK_W)
        else:
            assert keep == tm and seq_len % tm == 0
            tiles_per_seq = seq_len // tm
            kt_shape = (n // seq_len, TOK_W, keep)
            kt_spec = pl.BlockSpec((None, TOK_W, tm), lambda i: (i // tiles_per_seq, 0, 0))
            vt_shape = ((n // seq_len) * keep, TOK_W)
            vt_spec = pl.BlockSpec((tm, TOK_W), lambda i: (i // tiles_per_seq, 0))
        out_shape += [jax.ShapeDtypeStruct(kt_shape, F32), jax.ShapeDtypeStruct(vt_shape, F32),
                      jax.ShapeDtypeStruct(k16_shape, BF16), jax.ShapeDtypeStruct((n, TOK_W), BF16)]
        out_specs += [kt_spec, vt_spec, k16_spec, row(TOK_W)]
    d_ff = wd.shape[0]
    return pl.pallas_call(
        functools.partial(_post_kernel, tail=tail),
        grid=(n // tm,),
        in_specs=in_specs,
        out_specs=out_specs,
        out_shape=out_shape,
        scratch_shapes=[pltpu.VMEM((tm, D_MODEL), BF16), pltpu.VMEM((tm, d_ff), BF16)],
        compiler_params=pltpu.CompilerParams(dimension_semantics=("arbitrary",),
                                             vmem_limit_bytes=VMEM_LIMIT),
        name="post_" + tail,
    )(x, o_tok, o_mem, *consts)


def _memkv_kernel(m_ref, g_ref, wk_ref, wv_ref, k_ref, v_ref):
    h = _rms(m_ref[...], g_ref[...]).astype(BF16)
    kt = _dot_nt(wk_ref[...], h)
    vt = _dot_nt(wv_ref[...], h)
    n_mem = k_ref.shape[-1]
    for s in range(k_ref.shape[0]):
        k_ref[s] = kt[:, s * n_mem:(s + 1) * n_mem]
        v_ref[s] = vt[:, s * n_mem:(s + 1) * n_mem]


def _memkv_call(mem, g_mem, wkt, wvt):
    b, n_mem, _ = mem.shape
    depth = g_mem.shape[0]
    tm = _token_tile(b * n_mem)
    assert tm % n_mem == 0
    per_tile = tm // n_mem
    w_spec = pl.BlockSpec((None, MEM_W, D_MODEL), lambda l, i: (l, 0, 0))
    o_spec = pl.BlockSpec((None, per_tile, MEM_W, n_mem), lambda l, i: (l, i, 0, 0))
    return pl.pallas_call(
        _memkv_kernel,
        grid=(depth, b // per_tile),
        in_specs=[pl.BlockSpec((tm, D_MODEL), lambda l, i: (i, 0)),
                  pl.BlockSpec((None, 1, D_MODEL), lambda l, i: (l, 0, 0)), w_spec, w_spec],
        out_specs=[o_spec, o_spec],
        out_shape=[jax.ShapeDtypeStruct((depth, b, MEM_W, n_mem), F32)] * 2,
        compiler_params=pltpu.CompilerParams(dimension_semantics=("arbitrary", "arbitrary")),
        name="mem_kv",
    )(mem.reshape(b * n_mem, D_MODEL), g_mem, wkt, wvt)


def _sb_kernel(q_ref, k_ref, v_ref, o_ref, kb_ref, vb_ref, qs_ref, acc_ref, *, q_start):
    i = pl.program_id(2)
    tq, width = q_ref.shape
    tk = SB_K_BLOCK
    n_heads = width // HEAD_DIM

    @pl.when(i == 0)
    def _():
        for blk in range(kb_ref.shape[0]):
            kb_ref[blk] = k_ref[:, blk * tk:(blk + 1) * tk].astype(BF16)
            vb_ref[blk] = v_ref[:, blk * tk:(blk + 1) * tk].astype(BF16)

    for h, qh in enumerate(_head_split(q_ref[...], n_heads)):
        qs_ref[h * tq:(h + 1) * tq, :] = qh
    rows = n_heads * tq
    q0 = q_start + i * tq
    u_after = jnp.where(lax.broadcasted_iota(jnp.int32, (tk, tk), 0)
                        > lax.broadcasted_iota(jnp.int32, (tk, tk), 1), 1.0, 0.0).astype(BF16)
    n_blocks = jnp.minimum((q0 + tq - 2) // tk + 1, kb_ref.shape[0])

    def sweep_block(t, run, newest):
        blk = n_blocks - 1 - t
        ks = blk * tk
        z = _dot(qs_ref[...], kb_ref[blk]) * LOG2E
        sp = jnp.maximum(z, 0.0) + jnp.log2(1.0 + jnp.exp2(_neg_abs(z)))
        if newest:
            q_pos = q0 + (lax.broadcasted_iota(jnp.int32, (rows, tk), 0) & (tq - 1))
            vis = ks + lax.broadcasted_iota(jnp.int32, (rows, tk), 1) < q_pos
            sp = jnp.where(vis, sp, 0.0)
        sp16 = sp.astype(BF16)
        after = _dot(sp16, u_after)
        if newest:
            w = jnp.where(vis, jnp.exp2((z - sp) - after), 0.0)
        else:
            w = jnp.exp2((z - sp) - after - run)
        w = w.astype(BF16)
        for p in range(n_heads // 2):
            pair_rows = slice(2 * p * tq, (2 * p + 2) * tq)
            pv = _dot_nt(w[pair_rows], vb_ref[blk, p * LANES:(p + 1) * LANES, :])
            if newest:
                acc_ref[pair_rows] = pv
            else:
                acc_ref[pair_rows] += pv
        return run + after[:, 0:1] + sp16[:, 0:1].astype(F32)

    def unfinished(run):
        return (jnp.min(run) < SB_EXIT).astype(jnp.int32)

    def cond(carry):
        t, go, _ = carry
        return jnp.logical_and(t < n_blocks, go > 0)

    def body(carry):
        t, _, run = carry
        run = sweep_block(t, run, False)
        return t + 1, unfinished(run), run

    run = sweep_block(0, jnp.zeros((rows, 1), F32), True)
    lax.while_loop(cond, body, (jnp.int32(1), unfinished(run), run))
    for p in range(n_heads // 2):
        r0 = 2 * p * tq
        o_ref[:, p * LANES:(p + 1) * LANES] = _head_merge(
            acc_ref[r0:r0 + tq], acc_ref[r0 + tq:r0 + 2 * tq]).astype(BF16)


def _sb_call(q, k, v, q_start, layer=None):
    b, t, _ = q.shape
    tk_all = k.shape[-1]
    tq = min(SB_Q_BLOCK, t)
    assert t % tq == 0 and SB_K_BLOCK % tq == 0 and q_start % tq == 0 and tk_all % SB_K_BLOCK == 0
    assert tq & (tq - 1) == 0
    q_spec = pl.BlockSpec((None, tq, SB_LANES), lambda bi, g, i: (bi, i, g))
    if layer is None:
        kv_spec = pl.BlockSpec((None, SB_LANES, tk_all), lambda bi, g, i: (bi, g, 0))
    else:
        kv_spec = pl.BlockSpec((None, None, SB_LANES, tk_all), lambda bi, g, i: (layer, bi, g, 0))
    kv_scratch = pltpu.VMEM((tk_all // SB_K_BLOCK, SB_LANES, SB_K_BLOCK), BF16)
    return pl.pallas_call(
        functools.partial(_sb_kernel, q_start=q_start),
        grid=(b, TOK_W // SB_LANES, t // tq),
        in_specs=[q_spec, kv_spec, kv_spec],
        out_specs=q_spec,
        out_shape=jax.ShapeDtypeStruct((b, t, TOK_W), BF16),
        scratch_shapes=[kv_scratch, kv_scratch,
                        pltpu.VMEM((SB_LANES // HEAD_DIM * tq, SB_LANES), BF16),
                        pltpu.VMEM((SB_LANES // HEAD_DIM * tq, LANES), F32)],
        compiler_params=pltpu.CompilerParams(dimension_semantics=("arbitrary",) * 3),
        name="stick_breaking",
    )(q, k, v)


def _toeplitz(line, rows, cols):
    period = rows + cols
    e = jnp.roll(jnp.pad(line, ((0, 0), (0, 1))), -(rows - 1), axis=1)
    tiled = jnp.tile(e, (1, rows))[:, :rows * (period - 1)]
    return tiled.reshape(line.shape[0], rows, period - 1)[:, :, :cols]


def _band_line(rel_bias, dist0, length):
    dist = dist0 - np.arange(length)
    return rel_bias[:, np.clip(dist, -REL_CLIP, REL_CLIP) + REL_CLIP]


def _band_visible(q_pos0, n_q, k_pos0, n_k, k_first):
    q_pos = q_pos0 + np.arange(n_q)
    k_pos = k_pos0 + np.arange(n_k)
    qc = q_pos[:, None] // CHUNK
    kc = k_pos[None, :] // CHUNK
    return (k_pos[None, :] >= k_first) & (kc <= qc) & (kc >= qc - BAND_CHUNKS)


def _band_heads(q_ref, kt_ref, v_ref, bm_ref, o_ref, keys, bias_cols):
    tq = q_ref.shape[0]
    for p in range(q_ref.shape[-1] // LANES):
        lanes = slice(p * LANES, (p + 1) * LANES)
        qs = jnp.concatenate(_head_split(q_ref[:, lanes], 2), axis=0)
        bias = bm_ref[2 * p:2 * p + 2, :, bias_cols].reshape(2 * tq, -1)
        o = _softmax_pv(_dot(qs, kt_ref[lanes, keys]) + bias, v_ref[keys, lanes])
        o_ref[:, lanes] = _head_merge(o[:tq], o[tq:]).astype(BF16)


def _band_prompt_kernel(q_ref, kt_ref, v_ref, bm_ref, o_ref):
    tq = BAND_Q_BLOCK
    edge = jnp.minimum(pl.program_id(1), BAND_EDGE) * tq
    start = pl.multiple_of(pl.program_id(1) * tq - edge, tq)
    first = pl.multiple_of(WINDOW_B - edge, tq)
    _band_heads(q_ref, kt_ref, v_ref, bm_ref, o_ref, pl.ds(start, BAND_WIN), pl.ds(first, BAND_WIN))


def _band_prompt_call(q, kt, v, rel_bias):
    b, t, _ = q.shape
    tq = BAND_Q_BLOCK
    assert t % tq == 0 and t >= BAND_WIN and WINDOW_B % tq == 0
    n_cols = BAND_WIN + BAND_EDGE * tq
    table = _toeplitz(_band_line(rel_bias, WINDOW_B + tq - 1, tq + n_cols - 1), tq, n_cols)
    bm = jnp.where(_band_visible(WINDOW_B, tq, 0, n_cols, 0)[None], table, NEG)
    q_spec = pl.BlockSpec((None, tq, TOK_W), lambda bi, i: (bi, i, 0))
    bm_spec = pl.BlockSpec(bm.shape, lambda bi, i: (0, 0, 0), pipeline_mode=pl.Buffered(1))
    return pl.pallas_call(
        _band_prompt_kernel,
        grid=(b, t // tq),
        in_specs=[q_spec, pl.BlockSpec((None, TOK_W, t), lambda bi, i: (bi, 0, 0)),
                  pl.BlockSpec((None, t, TOK_W), lambda bi, i: (bi, 0, 0)), bm_spec],
        out_specs=q_spec,
        out_shape=jax.ShapeDtypeStruct((b, t, TOK_W), BF16),
        compiler_params=pltpu.CompilerParams(dimension_semantics=("arbitrary",) * 2,
                                             vmem_limit_bytes=VMEM_LIMIT),
        name="band_prompt",
    )(q, kt, v, bm)


def _band_step_kernel(q_ref, kt_ref, v_ref, bm_ref, o_ref):
    _band_heads(q_ref, kt_ref, v_ref, bm_ref, o_ref, slice(None), slice(None))


def _band_step_call(q, kt_all, v_all, rel_bias, q_start):
    b, t, _ = q.shape
    k_first = q_start + t - v_all.shape[1]
    k_pad = (-v_all.shape[1]) % LANES
    kt_all = jnp.pad(kt_all, ((0, 0), (0, 0), (k_pad, 0)))
    v_all = jnp.pad(v_all, ((0, 0), (k_pad, 0), (0, 0)))
    tk = v_all.shape[1]
    k_pos0 = k_first - k_pad
    table = _toeplitz(_band_line(rel_bias, q_start - k_pos0 + t - 1, t + tk - 1), t, tk)
    bm = jnp.where(_band_visible(q_start, t, k_pos0, tk, k_first)[None], table, NEG)
    q_spec = pl.BlockSpec((None, t, TOK_W), lambda bi: (bi, 0, 0))
    return pl.pallas_call(
        _band_step_kernel,
        grid=(b,),
        in_specs=[q_spec, pl.BlockSpec((None, TOK_W, tk), lambda bi: (bi, 0, 0)),
                  pl.BlockSpec((None, tk, TOK_W), lambda bi: (bi, 0, 0)), _const_spec(bm.shape)],
        out_specs=q_spec,
        out_shape=jax.ShapeDtypeStruct((b, t, TOK_W), BF16),
        compiler_params=pltpu.CompilerParams(dimension_semantics=("arbitrary",)),
        name="band_step",
    )(q, kt_all, v_all, bm)


def _split_cols(w, widths):
    out, c = [], 0
    for wd in widths:
        out.append(w[:, c:c + wd].astype(BF16))
        c += wd
    return out


def _prep_layers(params):
    depth = params["g_ff1"].shape[0]
    n_a = params["w_in_a"].shape[0]
    layers = []
    for l in range(depth):
        lay = {name: params[name][l].reshape(1, D_MODEL) for name in ("g_ff1", "g_mix", "g_ff2")}
        lay["ff1"] = (params["w_ff1_gu"][l].astype(BF16), params["w_ff1_down"][l].astype(BF16))
        lay["ff2"] = (params["w_ff2_gu"][l].astype(BF16), params["w_ff2_down"][l].astype(BF16))
        if l < n_a:
            wq, wk, wv, wqm = _split_cols(params["w_in_a"][l], (TOK_W, TOK_W, TOK_W, MEM_W))
            lay["w_in"] = [wq, wk.T, wv.T, wqm]
        else:
            lay["w_in"] = _split_cols(params["w_in_b"][l - n_a], (TOK_W, MEM_W))
            lay["rel_bias"] = params["rel_bias_b"][l - n_a]
        lay["w_out"] = (params["w_out"][l][:TOK_W].astype(BF16), params["w_out"][l][TOK_W:].astype(BF16))
        if l == n_a - 1:
            lay["tail"] = "kv"
            wkb, wvb = _split_cols(params["w_kv_b"], (TOK_W, TOK_W))
            lay["tail_args"] = [params["g_kv"].reshape(1, D_MODEL), wkb.T, wvb]
        elif l == depth - 1:
            lay["tail"], lay["tail_args"] = "final", [params["g_final"].reshape(1, D_MODEL)]
        else:
            lay["tail"], lay["tail_args"] = "none", []
        layers.append(lay)
    return layers, n_a


def _per_stream(a, bn, t):
    if a.shape[-1] == t:
        return a
    return jnp.moveaxis(a.reshape(a.shape[:-1] + (bn, t)), -2, -3)


def _rows_to_features(a):
    return jnp.swapaxes(a.reshape(a.shape[:-2] + (a.shape[-2] * a.shape[-1],)), -1, -2)


def _features_to_heads(a):
    a = a.reshape(a.shape[:-2] + (a.shape[-2] // HEAD_DIM, HEAD_DIM, a.shape[-1]))
    return jnp.moveaxis(a, -1, -3)


def _run(x, mem_k, mem_v, past_a_k, past_a_v, past_b_k, past_b_v, layers, n_a):
    bn, t, _ = x.shape
    n = bn * t
    has_past = past_a_k is not None
    q_start = past_a_k.shape[2] if has_past else 0
    x = x.reshape(n, D_MODEL)
    k_stack = v_stack = kb_tail = vb_tail = kb16 = vb16 = None
    for l, lay in enumerate(layers):
        pre_args = (lay["g_ff1"], *lay["ff1"], lay["g_mix"], lay["w_in"], (l, mem_k), (l, mem_v))
        if l < n_a:
            x, q, k_stack, v_stack, o_mem = _pre_call(x, t, *pre_args, kv_slot=(l, n_a, k_stack, v_stack))
            q = q.reshape(bn, t, TOK_W)
            if has_past:
                pad = jnp.zeros((bn, TOK_W, (-(q_start + t)) % SB_K_BLOCK), F32)
                k = jnp.concatenate([_rows_to_features(past_a_k[l]), _per_stream(k_stack[l], bn, t), pad], 2)
                v = jnp.concatenate([_rows_to_features(past_a_v[l]), _per_stream(v_stack[l], bn, t), pad], 2)
                o_tok = _sb_call(q, k, v, q_start)
            else:
                o_tok = _sb_call(q, k_stack, v_stack, q_start, layer=l)
        else:
            x, q, o_mem = _pre_call(x, t, *pre_args)
            q = q.reshape(bn, t, TOK_W)
            kt16 = _per_stream(kb16, bn, t)
            v16 = vb16.reshape(bn, t, TOK_W)
            if has_past:
                kt_all = jnp.concatenate([_rows_to_features(past_b_k).astype(BF16), kt16], axis=2)
                v_all = jnp.concatenate([past_b_v.reshape(bn, -1, TOK_W).astype(BF16), v16], axis=1)
                o_tok = _band_step_call(q, kt_all, v_all, lay["rel_bias"], q_start)
            else:
                o_tok = _band_prompt_call(q, kt16, v16, lay["rel_bias"])
        outs = _post_call(x, t, o_tok.reshape(n, TOK_W), o_mem, *lay["w_out"], lay["g_ff2"], *lay["ff2"],
                          lay["tail"], lay["tail_args"])
        x = outs[0]
        if lay["tail"] == "kv":
            _, kb_tail, vb_tail, kb16, vb16 = outs
    return (x.reshape(bn, t, D_MODEL),
            _features_to_heads(_per_stream(k_stack, bn, t)), _features_to_heads(_per_stream(v_stack, bn, t)),
            _features_to_heads(_per_stream(kb_tail, bn, min(WINDOW_B, t))),
            vb_tail.reshape(bn, -1, N_HEADS_TOK, HEAD_DIM))


def kernel(x_prompt, x_sample, cache_a_k, cache_a_v, cache_b_k, cache_b_v, cache_mem_k, cache_mem_v,
           mem_prompt, g_ff1, w_ff1_gu, w_ff1_down, g_mix, w_in_a, w_in_b, w_out, g_mem, w_mem_kv,
           g_kv, w_kv_b, rel_bias_b, g_ff2, w_ff2_gu, w_ff2_down, g_final):
    params = dict(g_ff1=g_ff1, w_ff1_gu=w_ff1_gu, w_ff1_down=w_ff1_down, g_mix=g_mix, w_in_a=w_in_a,
                  w_in_b=w_in_b, w_out=w_out, g_kv=g_kv, w_kv_b=w_kv_b, rel_bias_b=rel_bias_b,
                  g_ff2=g_ff2, w_ff2_gu=w_ff2_gu, w_ff2_down=w_ff2_down, g_final=g_final)
    depth = g_mem.shape[0]
    bn, n_mem, _ = mem_prompt.shape

    w_mem_t = jnp.swapaxes(w_mem_kv, 1, 2).astype(BF16)
    mk, mv = _memkv_call(mem_prompt, g_mem.reshape(depth, 1, D_MODEL), w_mem_t[:, :MEM_W], w_mem_t[:, MEM_W:])

    layers, n_a = _prep_layers(params)
    y_prompt, a_k_prompt, a_v_prompt, b_k_prompt, b_v_prompt = _run(
        x_prompt, mk, mv, None, None, None, None, layers, n_a)

    y_sample, a_k_sample, a_v_sample, b_k_sample, b_v_sample = _run(
        x_sample, _rows_to_features(cache_mem_k), _rows_to_features(cache_mem_v),
        cache_a_k, cache_a_v, cache_b_k, cache_b_v, layers, n_a)

    return (y_prompt, y_sample, a_k_prompt, a_v_prompt, b_k_prompt, b_v_prompt,
            _features_to_heads(mk), _features_to_heads(mv), a_k_sample, a_v_sample, b_k_sample, b_v_sample)
```

```python
import functools

import numpy as np
import jax
import jax.numpy as jnp
from jax import lax
from jax.experimental import pallas as pl
from jax.experimental.pallas import tpu as pltpu

F32 = jnp.float32
BF16 = jnp.bfloat16

D_MODEL = 1024
HEAD_DIM = 64
CHUNK = 64
N_MEM = 256
N_HEADS_MEM = 4
N_HEADS_TOK = 12
TOK_W = N_HEADS_TOK * HEAD_DIM
MEM_W = N_HEADS_MEM * HEAD_DIM
BAND_CHUNKS = 8
WINDOW_B = BAND_CHUNKS * CHUNK
REL_CLIP = 128
EPS = 1e-6
NEG = -1e30
QK_SCALE = HEAD_DIM ** -0.5
LOG2E = 1.4426950408889634

LANES = 128
N_PAIRS_TOK = TOK_W // LANES
N_PAIRS_MEM = MEM_W // LANES
FF_CHUNK = 256
TOKEN_TILE = 512
SB_Q_BLOCK = 256
SB_K_BLOCK = 256
SB_GROUP = 2 * LANES
SB_LANES = TOK_W
SB_EXIT = 160.0
BAND_Q_BLOCK = 4 * CHUNK
BAND_WIN = WINDOW_B + BAND_Q_BLOCK
BAND_EDGE = WINDOW_B // BAND_Q_BLOCK
VMEM_LIMIT = 56 * 1024 * 1024


def _rms(x, g):
    return x * lax.rsqrt(jnp.mean(x * x, axis=-1, keepdims=True) + EPS) * g


def _dot(a, b):
    return jnp.dot(a, b, preferred_element_type=F32)


def _dot_nt(a, b):
    return lax.dot_general(a, b, (((1,), (1,)), ((), ())), preferred_element_type=F32)


def _neg_abs(x):
    return pltpu.bitcast(pltpu.bitcast(x, jnp.uint32) | jnp.uint32(0x80000000), F32)


def _head_split(x, n_heads):
    head = jnp.right_shift(lax.broadcasted_iota(jnp.int32, x.shape, 1), HEAD_DIM.bit_length() - 1)
    zero = jnp.zeros_like(x)
    return [jnp.where(head == h, x, zero) for h in range(n_heads)]


def _head_merge(o0, o1):
    lane = lax.broadcasted_iota(jnp.int32, o0.shape, 1)
    return jnp.where(lane < HEAD_DIM, o0, o1)


def _swiglu_residual(x_ref, g_ref, wgu_ref, wd_ref, h_ref, a_ref):
    h_ref[...] = _rms(x_ref[...], g_ref[...]).astype(BF16)
    f = wd_ref.shape[0]
    for lo in range(0, f, FF_CHUNK):
        gate = _dot(h_ref[...], wgu_ref[:, lo:lo + FF_CHUNK])
        up = _dot(h_ref[...], wgu_ref[:, f + lo:f + lo + FF_CHUNK])
        a_ref[:, lo:lo + FF_CHUNK] = (gate * (1.0 / (1.0 + jnp.exp(-gate))) * up).astype(BF16)
    return x_ref[...] + 0.5 * _dot(a_ref[...], wd_ref[...])


def _softmax_pv(s, v, v_feature_major=False):
    m = jnp.max(s, axis=-1, keepdims=True)
    e = jnp.exp(s - m)
    l = jnp.sum(e, axis=-1, keepdims=True)
    pv = _dot_nt(e.astype(BF16), v) if v_feature_major else _dot(e.astype(BF16), v)
    return pv / l


def _pre_kernel(*refs, has_kv, n_alias, n_streams):
    if has_kv:
        (x_ref, gff_ref, wgu_ref, wd_ref, gmix_ref, wq_ref, wk_ref, wv_ref, wqm_ref, mk_ref, mv_ref) = refs[:11]
        xo_ref, q_ref, k_ref, v_ref, om_ref, h_ref, a_ref = refs[11 + n_alias:]
    else:
        (x_ref, gff_ref, wgu_ref, wd_ref, gmix_ref, wq_ref, wqm_ref, mk_ref, mv_ref,
         xo_ref, q_ref, om_ref, h_ref, a_ref) = refs
    xo_ref[...] = _swiglu_residual(x_ref, gff_ref, wgu_ref, wd_ref, h_ref, a_ref)
    h_ref[...] = _rms(xo_ref[...], gmix_ref[...]).astype(BF16)
    q_ref[...] = (_dot(h_ref[...], wq_ref[...]) * QK_SCALE).astype(BF16)
    if has_kv:
        k_ref[...] = _dot_nt(wk_ref[...], h_ref[...])
        v_ref[...] = _dot_nt(wv_ref[...], h_ref[...])
    qm = (_dot(h_ref[...], wqm_ref[...]) * QK_SCALE).astype(BF16)
    rows = x_ref.shape[0] // n_streams
    for s in range(n_streams):
        r0 = s * rows
        for p in range(N_PAIRS_MEM):
            c0 = p * LANES
            mk = mk_ref[s, c0:c0 + LANES, :].astype(BF16)
            mv = mv_ref[s, c0:c0 + LANES, :].astype(BF16)
            qa, qb = _head_split(qm[r0:r0 + rows, c0:c0 + LANES], 2)
            oa = _softmax_pv(_dot(qa, mk), mv, v_feature_major=True)
            ob = _softmax_pv(_dot(qb, mk), mv, v_feature_major=True)
            om_ref[r0:r0 + rows, c0:c0 + LANES] = _head_merge(oa, ob).astype(BF16)


def _const_spec(shape):
    nd = len(shape)
    return pl.BlockSpec(shape, lambda i: (0,) * nd, pipeline_mode=pl.Buffered(1))


def _token_tile(n):
    tm = min(TOKEN_TILE, n)
    assert n % tm == 0
    return tm


def _feature_major(lead, n, seq_len, tm, width=TOK_W):
    lead_shape = tuple(s for s, _ in lead)
    lead_idx = tuple(ix for _, ix in lead)
    squeezed = (None,) * len(lead)
    if tm <= seq_len:
        tiles_per_seq = seq_len // tm
        return (lead_shape + (n // seq_len, width, seq_len),
                pl.BlockSpec(squeezed + (None, width, tm),
                             lambda i: lead_idx + (i // tiles_per_seq, 0, i % tiles_per_seq)))
    return (lead_shape + (width, n), pl.BlockSpec(squeezed + (width, tm), lambda i: lead_idx + (0, i)))


def _pre_call(x, seq_len, g_ff, wgu, wd, g_mix, w_in_parts, mem_k, mem_v, kv_slot=None):
    n = x.shape[0]
    tm = _token_tile(n)
    has_kv = kv_slot is not None
    mem_layer, mem_k = mem_k
    _, mem_v = mem_v
    if tm >= seq_len:
        assert tm % seq_len == 0
        n_streams = tm // seq_len
        mem_map = lambda i: (mem_layer, i, 0, 0)
    else:
        assert seq_len % tm == 0
        n_streams = 1
        tiles_per_seq = seq_len // tm
        mem_map = lambda i: (mem_layer, i // tiles_per_seq, 0, 0)
    row = lambda w: pl.BlockSpec((tm, w), lambda i: (i, 0))
    mem_spec = pl.BlockSpec((None, n_streams, MEM_W, N_MEM), mem_map)
    args = [x, g_ff, wgu, wd, g_mix, *w_in_parts, mem_k, mem_v]
    in_specs = [row(D_MODEL)] + [_const_spec(a.shape) for a in args[1:-2]] + [mem_spec, mem_spec]
    out_shape = [jax.ShapeDtypeStruct((n, D_MODEL), F32), jax.ShapeDtypeStruct((n, TOK_W), BF16)]
    out_specs = [row(D_MODEL), row(TOK_W)]
    aliases = {}
    n_alias = 0
    if has_kv:
        layer, n_layers, k_stack, v_stack = kv_slot
        kv_shape, kv_spec = _feature_major(((n_layers, layer),), n, seq_len, tm)
        out_shape += [jax.ShapeDtypeStruct(kv_shape, F32)] * 2
        out_specs += [kv_spec] * 2
        if k_stack is not None:
            n_alias = 2
            aliases = {len(args): 2, len(args) + 1: 3}
            args += [k_stack, v_stack]
            in_specs += [pl.BlockSpec(memory_space=pl.ANY)] * 2
    out_shape.append(jax.ShapeDtypeStruct((n, MEM_W), BF16))
    out_specs.append(row(MEM_W))
    d_ff = wd.shape[0]
    return pl.pallas_call(
        functools.partial(_pre_kernel, has_kv=has_kv, n_alias=n_alias, n_streams=n_streams),
        grid=(n // tm,),
        in_specs=in_specs,
        out_specs=out_specs,
        out_shape=out_shape,
        input_output_aliases=aliases,
        scratch_shapes=[pltpu.VMEM((tm, D_MODEL), BF16), pltpu.VMEM((tm, d_ff), BF16)],
        compiler_params=pltpu.CompilerParams(dimension_semantics=("arbitrary",),
                                             vmem_limit_bytes=VMEM_LIMIT),
        name="pre_kv" if has_kv else "pre_q",
    )(*args)


def _post_kernel(*refs, tail):
    if tail == "kv":
        (x_ref, ot_ref, om_ref, wot_ref, wom_ref, gff_ref, wgu_ref, wd_ref, gt_ref, wkb_ref, wvb_ref,
         xo_ref, kt_ref, vt_ref, k16_ref, v16_ref, h_ref, a_ref) = refs
    elif tail == "final":
        (x_ref, ot_ref, om_ref, wot_ref, wom_ref, gff_ref, wgu_ref, wd_ref, gt_ref,
         xo_ref, h_ref, a_ref) = refs
    else:
        (x_ref, ot_ref, om_ref, wot_ref, wom_ref, gff_ref, wgu_ref, wd_ref,
         xo_ref, h_ref, a_ref) = refs
    xo_ref[...] = x_ref[...] + _dot(ot_ref[...], wot_ref[...]) + _dot(om_ref[...], wom_ref[...])
    xo_ref[...] = _swiglu_residual(xo_ref, gff_ref, wgu_ref, wd_ref, h_ref, a_ref)
    if tail == "final":
        xo_ref[...] = _rms(xo_ref[...], gt_ref[...])
    if tail == "kv":
        h_ref[...] = _rms(xo_ref[...], gt_ref[...]).astype(BF16)
        kb = _dot_nt(wkb_ref[...], h_ref[...])
        vb = _dot(h_ref[...], wvb_ref[...])
        kt_ref[...] = kb
        vt_ref[...] = vb
        k16_ref[...] = kb.astype(BF16)
        v16_ref[...] = vb.astype(BF16)


def _post_call(x, seq_len, o_tok, o_mem, w_out_tok, w_out_mem, g_ff, wgu, wd, tail, tail_args):
    n = x.shape[0]
    tm = _token_tile(n)
    row = lambda w: pl.BlockSpec((tm, w), lambda i: (i, 0))
    consts = [w_out_tok, w_out_mem, g_ff, wgu, wd, *tail_args]
    in_specs = [row(D_MODEL), row(TOK_W), row(MEM_W)] + [_const_spec(c.shape) for c in consts]
    out_shape = [jax.ShapeDtypeStruct((n, D_MODEL), F32)]
    out_specs = [row(D_MODEL)]
    if tail == "kv":
        keep = min(WINDOW_B, seq_len)
        k16_shape, k16_spec = _feature_major((), n, seq_len, tm)
        if keep == seq_len:
            kt_shape, kt_spec, vt_shape, vt_spec = k16_shape, k16_spec, (n, TOK_W), row(TOK_W)
        else:
            assert keep == tm and seq_len % tm == 0
            tiles_per_seq = seq_len // tm
            kt_shape = (n // seq_len, TOK_W, keep)
            kt_spec = pl.BlockSpec((None, TOK_W, tm), lambda i: (i // tiles_per_seq, 0, 0))
            vt_shape = ((n // seq_len) * keep, TOK_W)
            vt_spec = pl.BlockSpec((tm, TOK_W), lambda i: (i // tiles_per_seq, 0))
        out_shape += [jax.ShapeDtypeStruct(kt_shape, F32), jax.ShapeDtypeStruct(vt_shape, F32),
                      jax.ShapeDtypeStruct(k16_shape, BF16), jax.ShapeDtypeStruct((n, TOK_W), BF16)]
        out_specs += [kt_spec, vt_spec, k16_spec, row(TOK_W)]
    d_ff = wd.shape[0]
    return pl.pallas_call(
        functools.partial(_post_kernel, tail=tail),
        grid=(n // tm,),
        in_specs=in_specs,
        out_specs=out_specs,
        out_shape=out_shape,
        scratch_shapes=[pltpu.VMEM((tm, D_MODEL), BF16), pltpu.VMEM((tm, d_ff), BF16)],
        compiler_params=pltpu.CompilerParams(dimension_semantics=("arbitrary",),
                                             vmem_limit_bytes=VMEM_LIMIT),
        name="post_" + tail,
    )(x, o_tok, o_mem, *consts)


def _memkv_kernel(m_ref, g_ref, wk_ref, wv_ref, k_ref, v_ref):
    h = _rms(m_ref[...], g_ref[...]).astype(BF16)
    kt = _dot_nt(wk_ref[...], h)
    vt = _dot_nt(wv_ref[...], h)
    n_mem = k_ref.shape[-1]
    for s in range(k_ref.shape[0]):
        k_ref[s] = kt[:, s * n_mem:(s + 1) * n_mem]
        v_ref[s] = vt[:, s * n_mem:(s + 1) * n_mem]


def _memkv_call(mem, g_mem, wkt, wvt):
    b, n_mem, _ = mem.shape
    depth = g_mem.shape[0]
    tm = _token_tile(b * n_mem)
    assert tm % n_mem == 0
    per_tile = tm // n_mem
    w_spec = pl.BlockSpec((None, MEM_W, D_MODEL), lambda l, i: (l, 0, 0))
    o_spec = pl.BlockSpec((None, per_tile, MEM_W, n_mem), lambda l, i: (l, i, 0, 0))
    return pl.pallas_call(
        _memkv_kernel,
        grid=(depth, b // per_tile),
        in_specs=[pl.BlockSpec((tm, D_MODEL), lambda l, i: (i, 0)),
                  pl.BlockSpec((None, 1, D_MODEL), lambda l, i: (l, 0, 0)), w_spec, w_spec],
        out_specs=[o_spec, o_spec],
        out_shape=[jax.ShapeDtypeStruct((depth, b, MEM_W, n_mem), F32)] * 2,
        compiler_params=pltpu.CompilerParams(dimension_semantics=("arbitrary", "arbitrary")),
        name="mem_kv",
    )(mem.reshape(b * n_mem, D_MODEL), g_mem, wkt, wvt)


def _sb_kernel(q_ref, k_ref, v_ref, o_ref, kb_ref, vb_ref, qs_ref, acc_ref, *, q_start):
    i = pl.program_id(2)
    tq, width = q_ref.shape
    tk = SB_K_BLOCK
    n_heads = width // HEAD_DIM

    @pl.when(i == 0)
    def _():
        for blk in range(kb_ref.shape[0]):
            kb_ref[blk] = k_ref[:, blk * tk:(blk + 1) * tk].astype(BF16)
            vb_ref[blk] = v_ref[:, blk * tk:(blk + 1) * tk].astype(BF16)

    heads_per_group = SB_GROUP // HEAD_DIM
    group_rows = heads_per_group * tq
    for g in range(width // SB_GROUP):
        for h, qh in enumerate(_head_split(q_ref[:, g * SB_GROUP:(g + 1) * SB_GROUP], heads_per_group)):
            r0 = g * group_rows + h * tq
            qs_ref[r0:r0 + tq, :] = qh
    rows = n_heads * tq
    q0 = q_start + i * tq
    u_after = jnp.where(lax.broadcasted_iota(jnp.int32, (tk, tk), 0)
                        > lax.broadcasted_iota(jnp.int32, (tk, tk), 1), 1.0, 0.0).astype(BF16)
    n_blocks = jnp.minimum((q0 + tq - 2) // tk + 1, kb_ref.shape[0])

    def sweep_block(t, run, newest):
        blk = n_blocks - 1 - t
        ks = blk * tk
        z = jnp.concatenate(
            [_dot(qs_ref[g * group_rows:(g + 1) * group_rows, :], kb_ref[blk, g * SB_GROUP:(g + 1) * SB_GROUP, :])
             for g in range(width // SB_GROUP)], axis=0) * LOG2E
        sp = jnp.maximum(z, 0.0) + jnp.log2(1.0 + jnp.exp2(_neg_abs(z)))
        if newest:
            q_pos = q0 + (lax.broadcasted_iota(jnp.int32, (rows, tk), 0) & (tq - 1))
            vis = ks + lax.broadcasted_iota(jnp.int32, (rows, tk), 1) < q_pos
            sp = jnp.where(vis, sp, 0.0)
        sp16 = sp.astype(BF16)
        after = _dot(sp16, u_after)
        if newest:
            w = jnp.where(vis, jnp.exp2((z - sp) - after), 0.0)
        else:
            w = jnp.exp2((z - sp) - after - run)
        w = w.astype(BF16)
        for p in range(n_heads // 2):
            pair_rows = slice(2 * p * tq, (2 * p + 2) * tq)
            pv = _dot_nt(w[pair_rows], vb_ref[blk, p * LANES:(p + 1) * LANES, :])
            if newest:
                acc_ref[pair_rows] = pv
            else:
                acc_ref[pair_rows] += pv
        return run + after[:, 0:1] + sp16[:, 0:1].astype(F32)

    def unfinished(run):
        return (jnp.min(run) < SB_EXIT).astype(jnp.int32)

    def cond(carry):
        t, go, _ = carry
        return jnp.logical_and(t < n_blocks, go > 0)

    def body(carry):
        t, _, run = carry
        run = sweep_block(t, run, False)
        return t + 1, unfinished(run), run

    run = sweep_block(0, jnp.zeros((rows, 1), F32), True)
    lax.while_loop(cond, body, (jnp.int32(1), unfinished(run), run))
    for p in range(n_heads // 2):
        r0 = 2 * p * tq
        o_ref[:, p * LANES:(p + 1) * LANES] = _head_merge(
            acc_ref[r0:r0 + tq], acc_ref[r0 + tq:r0 + 2 * tq]).astype(BF16)


def _sb_call(q, k, v, q_start, layer=None):
    b, t, _ = q.shape
    tk_all = k.shape[-1]
    tq = min(SB_Q_BLOCK, t)
    assert t % tq == 0 and SB_K_BLOCK % tq == 0 and q_start % tq == 0 and tk_all % SB_K_BLOCK == 0
    assert tq & (tq - 1) == 0
    q_spec = pl.BlockSpec((None, tq, SB_LANES), lambda bi, g, i: (bi, i, g))
    if layer is None:
        kv_spec = pl.BlockSpec((None, SB_LANES, tk_all), lambda bi, g, i: (bi, g, 0))
    else:
        kv_spec = pl.BlockSpec((None, None, SB_LANES, tk_all), lambda bi, g, i: (layer, bi, g, 0))
    kv_scratch = pltpu.VMEM((tk_all // SB_K_BLOCK, SB_LANES, SB_K_BLOCK), BF16)
    return pl.pallas_call(
        functools.partial(_sb_kernel, q_start=q_start),
        grid=(b, TOK_W // SB_LANES, t // tq),
        in_specs=[q_spec, kv_spec, kv_spec],
        out_specs=q_spec,
        out_shape=jax.ShapeDtypeStruct((b, t, TOK_W), BF16),
        scratch_shapes=[kv_scratch, kv_scratch,
                        pltpu.VMEM((SB_LANES // HEAD_DIM * tq, SB_GROUP), BF16),
                        pltpu.VMEM((SB_LANES // HEAD_DIM * tq, LANES), F32)],
        compiler_params=pltpu.CompilerParams(dimension_semantics=("arbitrary",) * 3,
                                             vmem_limit_bytes=VMEM_LIMIT),
        name="stick_breaking",
    )(q, k, v)


def _toeplitz(line, rows, cols):
    period = rows + cols
    e = jnp.roll(jnp.pad(line, ((0, 0), (0, 1))), -(rows - 1), axis=1)
    tiled = jnp.tile(e, (1, rows))[:, :rows * (period - 1)]
    return tiled.reshape(line.shape[0], rows, period - 1)[:, :, :cols]


def _band_line(rel_bias, dist0, length):
    dist = dist0 - np.arange(length)
    return rel_bias[:, np.clip(dist, -REL_CLIP, REL_CLIP) + REL_CLIP]


def _band_visible(q_pos0, n_q, k_pos0, n_k, k_first):
    q_pos = q_pos0 + np.arange(n_q)
    k_pos = k_pos0 + np.arange(n_k)
    qc = q_pos[:, None] // CHUNK
    kc = k_pos[None, :] // CHUNK
    return (k_pos[None, :] >= k_first) & (kc <= qc) & (kc >= qc - BAND_CHUNKS)


def _band_heads(q_ref, kt_ref, v_ref, bm_ref, o_ref, keys, bias_cols):
    tq = q_ref.shape[0]
    for p in range(q_ref.shape[-1] // LANES):
        lanes = slice(p * LANES, (p + 1) * LANES)
        qs = jnp.concatenate(_head_split(q_ref[:, lanes], 2), axis=0)
        bias = bm_ref[2 * p:2 * p + 2, :, bias_cols].reshape(2 * tq, -1)
        o = _softmax_pv(_dot(qs, kt_ref[lanes, keys]) + bias, v_ref[keys, lanes])
        o_ref[:, lanes] = _head_merge(o[:tq], o[tq:]).astype(BF16)


def _band_prompt_kernel(q_ref, kt_ref, v_ref, bm_ref, o_ref):
    tq = BAND_Q_BLOCK
    edge = jnp.minimum(pl.program_id(1), BAND_EDGE) * tq
    start = pl.multiple_of(pl.program_id(1) * tq - edge, tq)
    first = pl.multiple_of(WINDOW_B - edge, tq)
    _band_heads(q_ref, kt_ref, v_ref, bm_ref, o_ref, pl.ds(start, BAND_WIN), pl.ds(first, BAND_WIN))


def _band_prompt_call(q, kt, v, rel_bias):
    b, t, _ = q.shape
    tq = BAND_Q_BLOCK
    assert t % tq == 0 and t >= BAND_WIN and WINDOW_B % tq == 0
    n_cols = BAND_WIN + BAND_EDGE * tq
    table = _toeplitz(_band_line(rel_bias, WINDOW_B + tq - 1, tq + n_cols - 1), tq, n_cols)
    bm = jnp.where(_band_visible(WINDOW_B, tq, 0, n_cols, 0)[None], table, NEG)
    q_spec = pl.BlockSpec((None, tq, TOK_W), lambda bi, i: (bi, i, 0))
    bm_spec = pl.BlockSpec(bm.shape, lambda bi, i: (0, 0, 0), pipeline_mode=pl.Buffered(1))
    return pl.pallas_call(
        _band_prompt_kernel,
        grid=(b, t // tq),
        in_specs=[q_spec, pl.BlockSpec((None, TOK_W, t), lambda bi, i: (bi, 0, 0)),
                  pl.BlockSpec((None, t, TOK_W), lambda bi, i: (bi, 0, 0)), bm_spec],
        out_specs=q_spec,
        out_shape=jax.ShapeDtypeStruct((b, t, TOK_W), BF16),
        compiler_params=pltpu.CompilerParams(dimension_semantics=("arbitrary",) * 2,
                                             vmem_limit_bytes=VMEM_LIMIT),
        name="band_prompt",
    )(q, kt, v, bm)


def _band_step_kernel(q_ref, kt_ref, v_ref, bm_ref, o_ref):
    _band_heads(q_ref, kt_ref, v_ref, bm_ref, o_ref, slice(None), slice(None))


def _band_step_call(q, kt_all, v_all, rel_bias, q_start):
    b, t, _ = q.shape
    k_first = q_start + t - v_all.shape[1]
    k_pad = (-v_all.shape[1]) % LANES
    kt_all = jnp.pad(kt_all, ((0, 0), (0, 0), (k_pad, 0)))
    v_all = jnp.pad(v_all, ((0, 0), (k_pad, 0), (0, 0)))
    tk = v_all.shape[1]
    k_pos0 = k_first - k_pad
    table = _toeplitz(_band_line(rel_bias, q_start - k_pos0 + t - 1, t + tk - 1), t, tk)
    bm = jnp.where(_band_visible(q_start, t, k_pos0, tk, k_first)[None], table, NEG)
    q_spec = pl.BlockSpec((None, t, TOK_W), lambda bi: (bi, 0, 0))
    return pl.pallas_call(
        _band_step_kernel,
        grid=(b,),
        in_specs=[q_spec, pl.BlockSpec((None, TOK_W, tk), lambda bi: (bi, 0, 0)),
                  pl.BlockSpec((None, tk, TOK_W), lambda bi: (bi, 0, 0)), _const_spec(bm.shape)],
        out_specs=q_spec,
        out_shape=jax.ShapeDtypeStruct((b, t, TOK_W), BF16),
        compiler_params=pltpu.CompilerParams(dimension_semantics=("arbitrary",)),
        name="band_step",
    )(q, kt_all, v_all, bm)


def _split_cols(w, widths):
    out, c = [], 0
    for wd in widths:
        out.append(w[:, c:c + wd].astype(BF16))
        c += wd
    return out


def _prep_layers(params):
    depth = params["g_ff1"].shape[0]
    n_a = params["w_in_a"].shape[0]
    layers = []
    for l in range(depth):
        lay = {name: params[name][l].reshape(1, D_MODEL) for name in ("g_ff1", "g_mix", "g_ff2")}
        lay["ff1"] = (params["w_ff1_gu"][l].astype(BF16), params["w_ff1_down"][l].astype(BF16))
        lay["ff2"] = (params["w_ff2_gu"][l].astype(BF16), params["w_ff2_down"][l].astype(BF16))
        if l < n_a:
            wq, wk, wv, wqm = _split_cols(params["w_in_a"][l], (TOK_W, TOK_W, TOK_W, MEM_W))
            lay["w_in"] = [wq, wk.T, wv.T, wqm]
        else:
            lay["w_in"] = _split_cols(params["w_in_b"][l - n_a], (TOK_W, MEM_W))
            lay["rel_bias"] = params["rel_bias_b"][l - n_a]
        lay["w_out"] = (params["w_out"][l][:TOK_W].astype(BF16), params["w_out"][l][TOK_W:].astype(BF16))
        if l == n_a - 1:
            lay["tail"] = "kv"
            wkb, wvb = _split_cols(params["w_kv_b"], (TOK_W, TOK_W))
            lay["tail_args"] = [params["g_kv"].reshape(1, D_MODEL), wkb.T, wvb]
        elif l == depth - 1:
            lay["tail"], lay["tail_args"] = "final", [params["g_final"].reshape(1, D_MODEL)]
        else:
            lay["tail"], lay["tail_args"] = "none", []
        layers.append(lay)
    return layers, n_a


def _per_stream(a, bn, t):
    if a.shape[-1] == t:
        return a
    return jnp.moveaxis(a.reshape(a.shape[:-1] + (bn, t)), -2, -3)


def _rows_to_features(a):
    return jnp.swapaxes(a.reshape(a.shape[:-2] + (a.shape[-2] * a.shape[-1],)), -1, -2)


def _features_to_heads(a):
    a = a.reshape(a.shape[:-2] + (a.shape[-2] // HEAD_DIM, HEAD_DIM, a.shape[-1]))
    return jnp.moveaxis(a, -1, -3)


def _run(x, mem_k, mem_v, past_a_k, past_a_v, past_b_k, past_b_v, layers, n_a):
    bn, t, _ = x.shape
    n = bn * t
    has_past = past_a_k is not None
    q_start = past_a_k.shape[2] if has_past else 0
    x = x.reshape(n, D_MODEL)
    k_stack = v_stack = kb_tail = vb_tail = kb16 = vb16 = None
    for l, lay in enumerate(layers):
        pre_args = (lay["g_ff1"], *lay["ff1"], lay["g_mix"], lay["w_in"], (l, mem_k), (l, mem_v))
        if l < n_a:
            x, q, k_stack, v_stack, o_mem = _pre_call(x, t, *pre_args, kv_slot=(l, n_a, k_stack, v_stack))
            q = q.reshape(bn, t, TOK_W)
            if has_past:
                pad = jnp.zeros((bn, TOK_W, (-(q_start + t)) % SB_K_BLOCK), F32)
                k = jnp.concatenate([_rows_to_features(past_a_k[l]), _per_stream(k_stack[l], bn, t), pad], 2)
                v = jnp.concatenate([_rows_to_features(past_a_v[l]), _per_stream(v_stack[l], bn, t), pad], 2)
                o_tok = _sb_call(q, k, v, q_start)
            else:
                o_tok = _sb_call(q, k_stack, v_stack, q_start, layer=l)
        else:
            x, q, o_mem = _pre_call(x, t, *pre_args)
            q = q.reshape(bn, t, TOK_W)
            kt16 = _per_stream(kb16, bn, t)
            v16 = vb16.reshape(bn, t, TOK_W)
            if has_past:
                kt_all = jnp.concatenate([_rows_to_features(past_b_k).astype(BF16), kt16], axis=2)
                v_all = jnp.concatenate([past_b_v.reshape(bn, -1, TOK_W).astype(BF16), v16], axis=1)
                o_tok = _band_step_call(q, kt_all, v_all, lay["rel_bias"], q_start)
            else:
                o_tok = _band_prompt_call(q, kt16, v16, lay["rel_bias"])
        outs = _post_call(x, t, o_tok.reshape(n, TOK_W), o_mem, *lay["w_out"], lay["g_ff2"], *lay["ff2"],
                          lay["tail"], lay["tail_args"])
        x = outs[0]
        if lay["tail"] == "kv":
            _, kb_tail, vb_tail, kb16, vb16 = outs
    return (x.reshape(bn, t, D_MODEL),
            _features_to_heads(_per_stream(k_stack, bn, t)), _features_to_heads(_per_stream(v_stack, bn, t)),
            _features_to_heads(_per_stream(kb_tail, bn, min(WINDOW_B, t))),
            vb_tail.reshape(bn, -1, N_HEADS_TOK, HEAD_DIM))


def kernel(x_prompt, x_sample, cache_a_k, cache_a_v, cache_b_k, cache_b_v, cache_mem_k, cache_mem_v,
           mem_prompt, g_ff1, w_ff1_gu, w_ff1_down, g_mix, w_in_a, w_in_b, w_out, g_mem, w_mem_kv,
           g_kv, w_kv_b, rel_bias_b, g_ff2, w_ff2_gu, w_ff2_down, g_final):
    params = dict(g_ff1=g_ff1, w_ff1_gu=w_ff1_gu, w_ff1_down=w_ff1_down, g_mix=g_mix, w_in_a=w_in_a,
                  w_in_b=w_in_b, w_out=w_out, g_kv=g_kv, w_kv_b=w_kv_b, rel_bias_b=rel_bias_b,
                  g_ff2=g_ff2, w_ff2_gu=w_ff2_gu, w_ff2_down=w_ff2_down, g_final=g_final)
    depth = g_mem.shape[0]
    bn, n_mem, _ = mem_prompt.shape

    w_mem_t = jnp.swapaxes(w_mem_kv, 1, 2).astype(BF16)
    mk, mv = _memkv_call(mem_prompt, g_mem.reshape(depth, 1, D_MODEL), w_mem_t[:, :MEM_W], w_mem_t[:, MEM_W:])

    layers, n_a = _prep_layers(params)
    y_prompt, a_k_prompt, a_v_prompt, b_k_prompt, b_v_prompt = _run(
        x_prompt, mk, mv, None, None, None, None, layers, n_a)

    y_sample, a_k_sample, a_v_sample, b_k_sample, b_v_sample = _run(
        x_sample, _rows_to_features(cache_mem_k), _rows_to_features(cache_mem_v),
        cache_a_k, cache_a_v, cache_b_k, cache_b_v, layers, n_a)

    return (y_prompt, y_sample, a_k_prompt, a_v_prompt, b_k_prompt, b_v_prompt,
            _features_to_heads(mk), _features_to_heads(mv), a_k_sample, a_v_sample, b_k_sample, b_v_sample)
```

```python
import functools

import numpy as np
import jax
import jax.numpy as jnp
from jax import lax
from jax.experimental import pallas as pl
from jax.experimental.pallas import tpu as pltpu

F32 = jnp.float32
BF16 = jnp.bfloat16

D_MODEL = 1024
HEAD_DIM = 64
CHUNK = 64
N_MEM = 256
N_HEADS_MEM = 4
N_HEADS_TOK = 12
TOK_W = N_HEADS_TOK * HEAD_DIM
MEM_W = N_HEADS_MEM * HEAD_DIM
BAND_CHUNKS = 8
WINDOW_B = BAND_CHUNKS * CHUNK
REL_CLIP = 128
EPS = 1e-6
NEG = -1e30
QK_SCALE = HEAD_DIM ** -0.5
LOG2E = 1.4426950408889634

LANES = 128
N_PAIRS_TOK = TOK_W // LANES
N_PAIRS_MEM = MEM_W // LANES
FF_CHUNK = 256
TOKEN_TILE = 512
SB_Q_BLOCK = 256
SB_K_BLOCK = 256
SB_GROUP = 2 * LANES
SB_LANES = TOK_W
SB_EXIT = 160.0
BAND_Q_BLOCK = 4 * CHUNK
BAND_WIN = WINDOW_B + BAND_Q_BLOCK
BAND_EDGE = WINDOW_B // BAND_Q_BLOCK
VMEM_LIMIT = 56 * 1024 * 1024


def _rms(x, g):
    return x * lax.rsqrt(jnp.mean(x * x, axis=-1, keepdims=True) + EPS) * g


def _dot(a, b):
    return jnp.dot(a, b, preferred_element_type=F32)


def _dot_nt(a, b):
    return lax.dot_general(a, b, (((1,), (1,)), ((), ())), preferred_element_type=F32)


def _neg_abs(x):
    return pltpu.bitcast(pltpu.bitcast(x, jnp.uint32) | jnp.uint32(0x80000000), F32)


def _head_split(x, n_heads):
    head = jnp.right_shift(lax.broadcasted_iota(jnp.int32, x.shape, 1), HEAD_DIM.bit_length() - 1)
    zero = jnp.zeros_like(x)
    return [jnp.where(head == h, x, zero) for h in range(n_heads)]


def _head_merge(o0, o1):
    lane = lax.broadcasted_iota(jnp.int32, o0.shape, 1)
    return jnp.where(lane < HEAD_DIM, o0, o1)


def _swiglu_residual(x_ref, g_ref, wgu_ref, wd_ref, h_ref, a_ref):
    h_ref[...] = _rms(x_ref[...], g_ref[...]).astype(BF16)
    f = wd_ref.shape[0]
    for lo in range(0, f, FF_CHUNK):
        gate = _dot(h_ref[...], wgu_ref[:, lo:lo + FF_CHUNK])
        up = _dot(h_ref[...], wgu_ref[:, f + lo:f + lo + FF_CHUNK])
        a_ref[:, lo:lo + FF_CHUNK] = (gate * (1.0 / (1.0 + jnp.exp(-gate))) * up).astype(BF16)
    return x_ref[...] + 0.5 * _dot(a_ref[...], wd_ref[...])


def _softmax_pv(s, v, v_feature_major=False):
    m = jnp.max(s, axis=-1, keepdims=True)
    e = jnp.exp(s - m)
    l = jnp.sum(e, axis=-1, keepdims=True)
    pv = _dot_nt(e.astype(BF16), v) if v_feature_major else _dot(e.astype(BF16), v)
    return pv / l


def _pre_kernel(*refs, has_kv, n_alias, n_streams):
    if has_kv:
        (x_ref, gff_ref, wgu_ref, wd_ref, gmix_ref, wq_ref, wk_ref, wv_ref, wqm_ref, mk_ref, mv_ref) = refs[:11]
        xo_ref, q_ref, k_ref, v_ref, om_ref, h_ref, a_ref = refs[11 + n_alias:]
    else:
        (x_ref, gff_ref, wgu_ref, wd_ref, gmix_ref, wq_ref, wqm_ref, mk_ref, mv_ref,
         xo_ref, q_ref, om_ref, h_ref, a_ref) = refs
    xo_ref[...] = _swiglu_residual(x_ref, gff_ref, wgu_ref, wd_ref, h_ref, a_ref)
    h_ref[...] = _rms(xo_ref[...], gmix_ref[...]).astype(BF16)
    q_ref[...] = (_dot(h_ref[...], wq_ref[...]) * QK_SCALE).astype(BF16)
    if has_kv:
        k_ref[...] = _dot_nt(wk_ref[...], h_ref[...])
        v_ref[...] = _dot_nt(wv_ref[...], h_ref[...])
    qm = (_dot(h_ref[...], wqm_ref[...]) * QK_SCALE).astype(BF16)
    rows = x_ref.shape[0] // n_streams
    for s in range(n_streams):
        r0 = s * rows
        for p in range(N_PAIRS_MEM):
            c0 = p * LANES
            mk = mk_ref[s, c0:c0 + LANES, :].astype(BF16)
            mv = mv_ref[s, c0:c0 + LANES, :].astype(BF16)
            qa, qb = _head_split(qm[r0:r0 + rows, c0:c0 + LANES], 2)
            oa = _softmax_pv(_dot(qa, mk), mv, v_feature_major=True)
            ob = _softmax_pv(_dot(qb, mk), mv, v_feature_major=True)
            om_ref[r0:r0 + rows, c0:c0 + LANES] = _head_merge(oa, ob).astype(BF16)


def _const_spec(shape):
    nd = len(shape)
    return pl.BlockSpec(shape, lambda *_: (0,) * nd, pipeline_mode=pl.Buffered(1))


def _token_tile(n):
    tm = min(TOKEN_TILE, n)
    assert n % tm == 0
    return tm


def _feature_major(lead, n, seq_len, tm, width=TOK_W):
    lead_shape = tuple(s for s, _ in lead)
    lead_idx = tuple(ix for _, ix in lead)
    squeezed = (None,) * len(lead)
    if tm <= seq_len:
        tiles_per_seq = seq_len // tm
        return (lead_shape + (n // seq_len, width, seq_len),
                pl.BlockSpec(squeezed + (None, width, tm),
                             lambda i: lead_idx + (i // tiles_per_seq, 0, i % tiles_per_seq)))
    return (lead_shape + (width, n), pl.BlockSpec(squeezed + (width, tm), lambda i: lead_idx + (0, i)))


def _pre_call(x, seq_len, g_ff, wgu, wd, g_mix, w_in_parts, mem_k, mem_v, kv_slot=None):
    n = x.shape[0]
    tm = _token_tile(n)
    has_kv = kv_slot is not None
    mem_layer, mem_k = mem_k
    _, mem_v = mem_v
    if tm >= seq_len:
        assert tm % seq_len == 0
        n_streams = tm // seq_len
        mem_map = lambda i: (mem_layer, i, 0, 0)
    else:
        assert seq_len % tm == 0
        n_streams = 1
        tiles_per_seq = seq_len // tm
        mem_map = lambda i: (mem_layer, i // tiles_per_seq, 0, 0)
    row = lambda w: pl.BlockSpec((tm, w), lambda i: (i, 0))
    mem_spec = pl.BlockSpec((None, n_streams, MEM_W, N_MEM), mem_map)
    args = [x, g_ff, wgu, wd, g_mix, *w_in_parts, mem_k, mem_v]
    in_specs = [row(D_MODEL)] + [_const_spec(a.shape) for a in args[1:-2]] + [mem_spec, mem_spec]
    out_shape = [jax.ShapeDtypeStruct((n, D_MODEL), F32), jax.ShapeDtypeStruct((n, TOK_W), BF16)]
    out_specs = [row(D_MODEL), row(TOK_W)]
    aliases = {}
    n_alias = 0
    if has_kv:
        layer, n_layers, k_stack, v_stack = kv_slot
        kv_shape, kv_spec = _feature_major(((n_layers, layer),), n, seq_len, tm)
        out_shape += [jax.ShapeDtypeStruct(kv_shape, F32)] * 2
        out_specs += [kv_spec] * 2
        if k_stack is not None:
            n_alias = 2
            aliases = {len(args): 2, len(args) + 1: 3}
            args += [k_stack, v_stack]
            in_specs += [pl.BlockSpec(memory_space=pl.ANY)] * 2
    out_shape.append(jax.ShapeDtypeStruct((n, MEM_W), BF16))
    out_specs.append(row(MEM_W))
    d_ff = wd.shape[0]
    return pl.pallas_call(
        functools.partial(_pre_kernel, has_kv=has_kv, n_alias=n_alias, n_streams=n_streams),
        grid=(n // tm,),
        in_specs=in_specs,
        out_specs=out_specs,
        out_shape=out_shape,
        input_output_aliases=aliases,
        scratch_shapes=[pltpu.VMEM((tm, D_MODEL), BF16), pltpu.VMEM((tm, d_ff), BF16)],
        compiler_params=pltpu.CompilerParams(dimension_semantics=("arbitrary",),
                                             vmem_limit_bytes=VMEM_LIMIT),
        name="pre_kv" if has_kv else "pre_q",
    )(*args)


def _post_kernel(*refs, tail):
    if tail == "kv":
        (x_ref, ot_ref, om_ref, wot_ref, wom_ref, gff_ref, wgu_ref, wd_ref, gt_ref, wkb_ref, wvb_ref,
         xo_ref, kt_ref, vt_ref, k16_ref, v16_ref, h_ref, a_ref) = refs
    elif tail == "final":
        (x_ref, ot_ref, om_ref, wot_ref, wom_ref, gff_ref, wgu_ref, wd_ref, gt_ref,
         xo_ref, h_ref, a_ref) = refs
    else:
        (x_ref, ot_ref, om_ref, wot_ref, wom_ref, gff_ref, wgu_ref, wd_ref,
         xo_ref, h_ref, a_ref) = refs
    xo_ref[...] = x_ref[...] + _dot(ot_ref[...], wot_ref[...]) + _dot(om_ref[...], wom_ref[...])
    xo_ref[...] = _swiglu_residual(xo_ref, gff_ref, wgu_ref, wd_ref, h_ref, a_ref)
    if tail == "final":
        xo_ref[...] = _rms(xo_ref[...], gt_ref[...])
    if tail == "kv":
        h_ref[...] = _rms(xo_ref[...], gt_ref[...]).astype(BF16)
        kb = _dot_nt(wkb_ref[...], h_ref[...])
        vb = _dot(h_ref[...], wvb_ref[...])
        kt_ref[...] = kb
        vt_ref[...] = vb
        k16_ref[...] = kb.astype(BF16)
        v16_ref[...] = vb.astype(BF16)


def _post_call(x, seq_len, o_tok, o_mem, w_out_tok, w_out_mem, g_ff, wgu, wd, tail, tail_args):
    n = x.shape[0]
    tm = _token_tile(n)
    row = lambda w: pl.BlockSpec((tm, w), lambda i: (i, 0))
    consts = [w_out_tok, w_out_mem, g_ff, wgu, wd, *tail_args]
    in_specs = [row(D_MODEL), row(TOK_W), row(MEM_W)] + [_const_spec(c.shape) for c in consts]
    out_shape = [jax.ShapeDtypeStruct((n, D_MODEL), F32)]
    out_specs = [row(D_MODEL)]
    if tail == "kv":
        keep = min(WINDOW_B, seq_len)
        k16_shape, k16_spec = _feature_major((), n, seq_len, tm)
        if keep == seq_len:
            kt_shape, kt_spec, vt_shape, vt_spec = k16_shape, k16_spec, (n, TOK_W), row(TOK_W)
        else:
            assert keep == tm and seq_len % tm == 0
            tiles_per_seq = seq_len // tm
            kt_shape = (n // seq_len, TOK_W, keep)
            kt_spec = pl.BlockSpec((None, TOK_W, tm), lambda i: (i // tiles_per_seq, 0, 0))
            vt_shape = ((n // seq_len) * keep, TOK_W)
            vt_spec = pl.BlockSpec((tm, TOK_W), lambda i: (i // tiles_per_seq, 0))
        out_shape += [jax.ShapeDtypeStruct(kt_shape, F32), jax.ShapeDtypeStruct(vt_shape, F32),
                      jax.ShapeDtypeStruct(k16_shape, BF16), jax.ShapeDtypeStruct((n, TOK_W), BF16)]
        out_specs += [kt_spec, vt_spec, k16_spec, row(TOK_W)]
    d_ff = wd.shape[0]
    return pl.pallas_call(
        functools.partial(_post_kernel, tail=tail),
        grid=(n // tm,),
        in_specs=in_specs,
        out_specs=out_specs,
        out_shape=out_shape,
        scratch_shapes=[pltpu.VMEM((tm, D_MODEL), BF16), pltpu.VMEM((tm, d_ff), BF16)],
        compiler_params=pltpu.CompilerParams(dimension_semantics=("arbitrary",),
                                             vmem_limit_bytes=VMEM_LIMIT),
        name="post_" + tail,
    )(x, o_tok, o_mem, *consts)


def _memkv_kernel(m_ref, g_ref, wk_ref, wv_ref, k_ref, v_ref):
    h = _rms(m_ref[...], g_ref[...]).astype(BF16)
    kt = _dot_nt(wk_ref[...], h)
    vt = _dot_nt(wv_ref[...], h)
    n_mem = k_ref.shape[-1]
    for s in range(k_ref.shape[0]):
        k_ref[s] = kt[:, s * n_mem:(s + 1) * n_mem]
        v_ref[s] = vt[:, s * n_mem:(s + 1) * n_mem]


def _memkv_call(mem, g_mem, wkt, wvt):
    b, n_mem, _ = mem.shape
    depth = g_mem.shape[0]
    tm = _token_tile(b * n_mem)
    assert tm % n_mem == 0
    per_tile = tm // n_mem
    w_spec = pl.BlockSpec((None, MEM_W, D_MODEL), lambda l, i: (l, 0, 0))
    o_spec = pl.BlockSpec((None, per_tile, MEM_W, n_mem), lambda l, i: (l, i, 0, 0))
    return pl.pallas_call(
        _memkv_kernel,
        grid=(depth, b // per_tile),
        in_specs=[pl.BlockSpec((tm, D_MODEL), lambda l, i: (i, 0)),
                  pl.BlockSpec((None, 1, D_MODEL), lambda l, i: (l, 0, 0)), w_spec, w_spec],
        out_specs=[o_spec, o_spec],
        out_shape=[jax.ShapeDtypeStruct((depth, b, MEM_W, n_mem), F32)] * 2,
        compiler_params=pltpu.CompilerParams(dimension_semantics=("arbitrary", "arbitrary")),
        name="mem_kv",
    )(mem.reshape(b * n_mem, D_MODEL), g_mem, wkt, wvt)


def _sb_kernel(q_ref, k_ref, v_ref, o_ref, kb_ref, vb_ref, qs_ref, acc_ref, *, q_start):
    i = pl.program_id(2)
    tq, width = q_ref.shape
    tk = SB_K_BLOCK
    n_heads = width // HEAD_DIM

    @pl.when(i == 0)
    def _():
        for blk in range(kb_ref.shape[0]):
            kb_ref[blk] = k_ref[:, blk * tk:(blk + 1) * tk].astype(BF16)
            vb_ref[blk] = v_ref[:, blk * tk:(blk + 1) * tk].astype(BF16)

    heads_per_group = SB_GROUP // HEAD_DIM
    group_rows = heads_per_group * tq
    for g in range(width // SB_GROUP):
        for h, qh in enumerate(_head_split(q_ref[:, g * SB_GROUP:(g + 1) * SB_GROUP], heads_per_group)):
            r0 = g * group_rows + h * tq
            qs_ref[r0:r0 + tq, :] = qh
    rows = n_heads * tq
    q0 = q_start + i * tq
    u_after = jnp.where(lax.broadcasted_iota(jnp.int32, (tk, tk), 0)
                        > lax.broadcasted_iota(jnp.int32, (tk, tk), 1), 1.0, 0.0).astype(BF16)
    n_blocks = jnp.minimum((q0 + tq - 2) // tk + 1, kb_ref.shape[0])

    def sweep_block(t, run, newest):
        blk = n_blocks - 1 - t
        ks = blk * tk
        z = jnp.concatenate(
            [_dot(qs_ref[g * group_rows:(g + 1) * group_rows, :], kb_ref[blk, g * SB_GROUP:(g + 1) * SB_GROUP, :])
             for g in range(width // SB_GROUP)], axis=0) * LOG2E
        sp = jnp.maximum(z, 0.0) + jnp.log2(1.0 + jnp.exp2(_neg_abs(z)))
        if newest:
            vis = (ks + lax.broadcasted_iota(jnp.int32, (tq, tk), 1)
                   < q0 + lax.broadcasted_iota(jnp.int32, (tq, tk), 0))[None]
            visible = lambda a: jnp.where(vis, a.reshape(n_heads, tq, tk), 0.0).reshape(rows, tk)
            sp = visible(sp)
        sp16 = sp.astype(BF16)
        after = _dot(sp16, u_after)
        if newest:
            w = visible(jnp.exp2((z - sp) - after))
        else:
            w = jnp.exp2((z - sp) - after - run)
        w = w.astype(BF16)
        for p in range(n_heads // 2):
            pair_rows = slice(2 * p * tq, (2 * p + 2) * tq)
            pv = _dot_nt(w[pair_rows], vb_ref[blk, p * LANES:(p + 1) * LANES, :])
            if newest:
                acc_ref[pair_rows] = pv
            else:
                acc_ref[pair_rows] += pv
        return run + after[:, 0:1] + sp16[:, 0:1].astype(F32)

    def unfinished(run):
        return (jnp.min(run) < SB_EXIT).astype(jnp.int32)

    def cond(carry):
        t, go, _ = carry
        return jnp.logical_and(t < n_blocks, go > 0)

    def body(carry):
        t, _, run = carry
        run = sweep_block(t, run, False)
        return t + 1, unfinished(run), run

    run = sweep_block(0, jnp.zeros((rows, 1), F32), True)
    lax.while_loop(cond, body, (jnp.int32(1), unfinished(run), run))
    for p in range(n_heads // 2):
        r0 = 2 * p * tq
        o_ref[:, p * LANES:(p + 1) * LANES] = _head_merge(
            acc_ref[r0:r0 + tq], acc_ref[r0 + tq:r0 + 2 * tq]).astype(BF16)


def _sb_call(q, k, v, q_start, layer=None):
    b, t, _ = q.shape
    tk_all = k.shape[-1]
    tq = min(SB_Q_BLOCK, t)
    assert t % tq == 0 and SB_K_BLOCK % tq == 0 and q_start % tq == 0 and tk_all % SB_K_BLOCK == 0
    q_spec = pl.BlockSpec((None, tq, SB_LANES), lambda bi, g, i: (bi, i, g))
    if layer is None:
        kv_spec = pl.BlockSpec((None, SB_LANES, tk_all), lambda bi, g, i: (bi, g, 0))
    else:
        kv_spec = pl.BlockSpec((None, None, SB_LANES, tk_all), lambda bi, g, i: (layer, bi, g, 0))
    kv_scratch = pltpu.VMEM((tk_all // SB_K_BLOCK, SB_LANES, SB_K_BLOCK), BF16)
    return pl.pallas_call(
        functools.partial(_sb_kernel, q_start=q_start),
        grid=(b, TOK_W // SB_LANES, t // tq),
        in_specs=[q_spec, kv_spec, kv_spec],
        out_specs=q_spec,
        out_shape=jax.ShapeDtypeStruct((b, t, TOK_W), BF16),
        scratch_shapes=[kv_scratch, kv_scratch,
                        pltpu.VMEM((SB_LANES // HEAD_DIM * tq, SB_GROUP), BF16),
                        pltpu.VMEM((SB_LANES // HEAD_DIM * tq, LANES), F32)],
        compiler_params=pltpu.CompilerParams(dimension_semantics=("arbitrary",) * 3,
                                             vmem_limit_bytes=VMEM_LIMIT),
        name="stick_breaking",
    )(q, k, v)


def _toeplitz(line, rows, cols):
    period = rows + cols
    e = jnp.roll(jnp.pad(line, ((0, 0), (0, 1))), -(rows - 1), axis=1)
    tiled = jnp.tile(e, (1, rows))[:, :rows * (period - 1)]
    return tiled.reshape(line.shape[0], rows, period - 1)[:, :, :cols]


def _band_line(rel_bias, dist0, length):
    dist = dist0 - np.arange(length)
    return rel_bias[:, np.clip(dist, -REL_CLIP, REL_CLIP) + REL_CLIP]


def _band_visible(q_pos0, n_q, k_pos0, n_k, k_first):
    q_pos = q_pos0 + np.arange(n_q)
    k_pos = k_pos0 + np.arange(n_k)
    qc = q_pos[:, None] // CHUNK
    kc = k_pos[None, :] // CHUNK
    return (k_pos[None, :] >= k_first) & (kc <= qc) & (kc >= qc - BAND_CHUNKS)


def _band_heads(q_ref, kt_ref, v_ref, bm_ref, o_ref, keys, bias_cols):
    tq = q_ref.shape[0]
    for p in range(q_ref.shape[-1] // LANES):
        lanes = slice(p * LANES, (p + 1) * LANES)
        qs = jnp.concatenate(_head_split(q_ref[:, lanes], 2), axis=0)
        bias = bm_ref[2 * p:2 * p + 2, :, bias_cols].reshape(2 * tq, -1)
        o = _softmax_pv(_dot(qs, kt_ref[lanes, keys]) + bias, v_ref[keys, lanes])
        o_ref[:, lanes] = _head_merge(o[:tq], o[tq:]).astype(BF16)


def _band_prompt_kernel(q_ref, kt_ref, v_ref, line_ref, o_ref, bm_ref):
    tq = BAND_Q_BLOCK
    n_cols = bm_ref.shape[-1]

    @pl.when(jnp.logical_and(pl.program_id(0) == 0, pl.program_id(1) == 0))
    def _():
        chunk_shift = CHUNK.bit_length() - 1
        r_chunk = jnp.right_shift(lax.broadcasted_iota(jnp.int32, (tq, n_cols), 0), chunk_shift)
        k_chunk = jnp.right_shift(lax.broadcasted_iota(jnp.int32, (tq, n_cols), 1), chunk_shift)
        vis = jnp.logical_and(k_chunk >= r_chunk, k_chunk <= r_chunk + BAND_CHUNKS)
        for h in range(bm_ref.shape[0]):
            line = jnp.broadcast_to(line_ref[h], (tq, line_ref.shape[-1]))
            rows = pltpu.roll(line, 0, 1, stride=1, stride_axis=0)
            bm_ref[h] = jnp.where(vis, rows[:, :n_cols], NEG)

    edge = jnp.minimum(pl.program_id(1), BAND_EDGE) * tq
    start = pl.multiple_of(pl.program_id(1) * tq - edge, tq)
    first = pl.multiple_of(WINDOW_B - edge, tq)
    _band_heads(q_ref, kt_ref, v_ref, bm_ref, o_ref, pl.ds(start, BAND_WIN), pl.ds(first, BAND_WIN))


def _band_prompt_call(q, kt, v, rel_bias):
    b, t, _ = q.shape
    tq = BAND_Q_BLOCK
    assert t % tq == 0 and t >= BAND_WIN and WINDOW_B % tq == 0
    n_cols = BAND_WIN + BAND_EDGE * tq
    period = n_cols + tq
    dist = WINDOW_B - (np.arange(period) + tq - 1) % period + tq - 1
    line = rel_bias[:, np.clip(dist, -REL_CLIP, REL_CLIP) + REL_CLIP].reshape(-1, 1, period)
    q_spec = pl.BlockSpec((None, tq, TOK_W), lambda bi, i: (bi, i, 0))
    return pl.pallas_call(
        _band_prompt_kernel,
        grid=(b, t // tq),
        in_specs=[q_spec, pl.BlockSpec((None, TOK_W, t), lambda bi, i: (bi, 0, 0)),
                  pl.BlockSpec((None, t, TOK_W), lambda bi, i: (bi, 0, 0)), _const_spec(line.shape)],
        out_specs=q_spec,
        out_shape=jax.ShapeDtypeStruct((b, t, TOK_W), BF16),
        scratch_shapes=[pltpu.VMEM((N_HEADS_TOK, tq, n_cols), F32)],
        compiler_params=pltpu.CompilerParams(dimension_semantics=("arbitrary",) * 2,
                                             vmem_limit_bytes=VMEM_LIMIT),
        name="band_prompt",
    )(q, kt, v, line)


def _band_step_kernel(q_ref, kt_ref, v_ref, bm_ref, o_ref):
    _band_heads(q_ref, kt_ref, v_ref, bm_ref, o_ref, slice(None), slice(None))


def _band_step_call(q, kt_all, v_all, rel_bias, q_start):
    b, t, _ = q.shape
    k_first = q_start + t - v_all.shape[1]
    k_pad = (-v_all.shape[1]) % LANES
    kt_all = jnp.pad(kt_all, ((0, 0), (0, 0), (k_pad, 0)))
    v_all = jnp.pad(v_all, ((0, 0), (k_pad, 0), (0, 0)))
    tk = v_all.shape[1]
    k_pos0 = k_first - k_pad
    table = _toeplitz(_band_line(rel_bias, q_start - k_pos0 + t - 1, t + tk - 1), t, tk)
    bm = jnp.where(_band_visible(q_start, t, k_pos0, tk, k_first)[None], table, NEG)
    q_spec = pl.BlockSpec((None, t, TOK_W), lambda bi: (bi, 0, 0))
    return pl.pallas_call(
        _band_step_kernel,
        grid=(b,),
        in_specs=[q_spec, pl.BlockSpec((None, TOK_W, tk), lambda bi: (bi, 0, 0)),
                  pl.BlockSpec((None, tk, TOK_W), lambda bi: (bi, 0, 0)), _const_spec(bm.shape)],
        out_specs=q_spec,
        out_shape=jax.ShapeDtypeStruct((b, t, TOK_W), BF16),
        compiler_params=pltpu.CompilerParams(dimension_semantics=("arbitrary",)),
        name="band_step",
    )(q, kt_all, v_all, bm)


def _split_cols(w, widths):
    out, c = [], 0
    for wd in widths:
        out.append(w[:, c:c + wd].astype(BF16))
        c += wd
    return out


def _prep_layers(params):
    depth = params["g_ff1"].shape[0]
    n_a = params["w_in_a"].shape[0]
    layers = []
    for l in range(depth):
        lay = {name: params[name][l].reshape(1, D_MODEL) for name in ("g_ff1", "g_mix", "g_ff2")}
        lay["ff1"] = (params["w_ff1_gu"][l].astype(BF16), params["w_ff1_down"][l].astype(BF16))
        lay["ff2"] = (params["w_ff2_gu"][l].astype(BF16), params["w_ff2_down"][l].astype(BF16))
        if l < n_a:
            wq, wk, wv, wqm = _split_cols(params["w_in_a"][l], (TOK_W, TOK_W, TOK_W, MEM_W))
            lay["w_in"] = [wq, wk.T, wv.T, wqm]
        else:
            lay["w_in"] = _split_cols(params["w_in_b"][l - n_a], (TOK_W, MEM_W))
            lay["rel_bias"] = params["rel_bias_b"][l - n_a]
        lay["w_out"] = (params["w_out"][l][:TOK_W].astype(BF16), params["w_out"][l][TOK_W:].astype(BF16))
        if l == n_a - 1:
            lay["tail"] = "kv"
            wkb, wvb = _split_cols(params["w_kv_b"], (TOK_W, TOK_W))
            lay["tail_args"] = [params["g_kv"].reshape(1, D_MODEL), wkb.T, wvb]
        elif l == depth - 1:
            lay["tail"], lay["tail_args"] = "final", [params["g_final"].reshape(1, D_MODEL)]
        else:
            lay["tail"], lay["tail_args"] = "none", []
        layers.append(lay)
    return layers, n_a


def _per_stream(a, bn, t):
    if a.shape[-1] == t:
        return a
    return jnp.moveaxis(a.reshape(a.shape[:-1] + (bn, t)), -2, -3)


def _rows_to_features(a):
    return jnp.swapaxes(a.reshape(a.shape[:-2] + (a.shape[-2] * a.shape[-1],)), -1, -2)


def _features_to_heads(a):
    a = a.reshape(a.shape[:-2] + (a.shape[-2] // HEAD_DIM, HEAD_DIM, a.shape[-1]))
    return jnp.moveaxis(a, -1, -3)


def _run(x, mem_k, mem_v, past_a_k, past_a_v, past_b_k, past_b_v, layers, n_a):
    bn, t, _ = x.shape
    n = bn * t
    has_past = past_a_k is not None
    q_start = past_a_k.shape[2] if has_past else 0
    x = x.reshape(n, D_MODEL)
    k_stack = v_stack = kb_tail = vb_tail = kb16 = vb16 = None
    for l, lay in enumerate(layers):
        pre_args = (lay["g_ff1"], *lay["ff1"], lay["g_mix"], lay["w_in"], (l, mem_k), (l, mem_v))
        if l < n_a:
            x, q, k_stack, v_stack, o_mem = _pre_call(x, t, *pre_args, kv_slot=(l, n_a, k_stack, v_stack))
            q = q.reshape(bn, t, TOK_W)
            if has_past:
                pad = jnp.zeros((bn, TOK_W, (-(q_start + t)) % SB_K_BLOCK), F32)
                k = jnp.concatenate([_rows_to_features(past_a_k[l]), _per_stream(k_stack[l], bn, t), pad], 2)
                v = jnp.concatenate([_rows_to_features(past_a_v[l]), _per_stream(v_stack[l], bn, t), pad], 2)
                o_tok = _sb_call(q, k, v, q_start)
            else:
                o_tok = _sb_call(q, k_stack, v_stack, q_start, layer=l)
        else:
            x, q, o_mem = _pre_call(x, t, *pre_args)
            q = q.reshape(bn, t, TOK_W)
            kt16 = _per_stream(kb16, bn, t)
            v16 = vb16.reshape(bn, t, TOK_W)
            if has_past:
                kt_all = jnp.concatenate([_rows_to_features(past_b_k).astype(BF16), kt16], axis=2)
                v_all = jnp.concatenate([past_b_v.reshape(bn, -1, TOK_W).astype(BF16), v16], axis=1)
                o_tok = _band_step_call(q, kt_all, v_all, lay["rel_bias"], q_start)
            else:
                o_tok = _band_prompt_call(q, kt16, v16, lay["rel_bias"])
        outs = _post_call(x, t, o_tok.reshape(n, TOK_W), o_mem, *lay["w_out"], lay["g_ff2"], *lay["ff2"],
                          lay["tail"], lay["tail_args"])
        x = outs[0]
        if lay["tail"] == "kv":
            _, kb_tail, vb_tail, kb16, vb16 = outs
    return (x.reshape(bn, t, D_MODEL),
            _features_to_heads(_per_stream(k_stack, bn, t)), _features_to_heads(_per_stream(v_stack, bn, t)),
            _features_to_heads(_per_stream(kb_tail, bn, min(WINDOW_B, t))),
            vb_tail.reshape(bn, -1, N_HEADS_TOK, HEAD_DIM))


def kernel(x_prompt, x_sample, cache_a_k, cache_a_v, cache_b_k, cache_b_v, cache_mem_k, cache_mem_v,
           mem_prompt, g_ff1, w_ff1_gu, w_ff1_down, g_mix, w_in_a, w_in_b, w_out, g_mem, w_mem_kv,
           g_kv, w_kv_b, rel_bias_b, g_ff2, w_ff2_gu, w_ff2_down, g_final):
    params = dict(g_ff1=g_ff1, w_ff1_gu=w_ff1_gu, w_ff1_down=w_ff1_down, g_mix=g_mix, w_in_a=w_in_a,
                  w_in_b=w_in_b, w_out=w_out, g_kv=g_kv, w_kv_b=w_kv_b, rel_bias_b=rel_bias_b,
                  g_ff2=g_ff2, w_ff2_gu=w_ff2_gu, w_ff2_down=w_ff2_down, g_final=g_final)
    depth = g_mem.shape[0]
    bn, n_mem, _ = mem_prompt.shape

    w_mem_t = jnp.swapaxes(w_mem_kv, 1, 2).astype(BF16)
    mk, mv = _memkv_call(mem_prompt, g_mem.reshape(depth, 1, D_MODEL), w_mem_t[:, :MEM_W], w_mem_t[:, MEM_W:])

    layers, n_a = _prep_layers(params)
    y_prompt, a_k_prompt, a_v_prompt, b_k_prompt, b_v_prompt = _run(
        x_prompt, mk, mv, None, None, None, None, layers, n_a)

    y_sample, a_k_sample, a_v_sample, b_k_sample, b_v_sample = _run(
        x_sample, _rows_to_features(cache_mem_k), _rows_to_features(cache_mem_v),
        cache_a_k, cache_a_v, cache_b_k, cache_b_v, layers, n_a)

    return (y_prompt, y_sample, a_k_prompt, a_v_prompt, b_k_prompt, b_v_prompt,
            _features_to_heads(mk), _features_to_heads(mv), a_k_sample, a_v_sample, b_k_sample, b_v_sample)
```

```python
import functools

import numpy as np
import jax
import jax.numpy as jnp
from jax import lax
from jax.experimental import pallas as pl
from jax.experimental.pallas import tpu as pltpu

F32 = jnp.float32
BF16 = jnp.bfloat16

D_MODEL = 1024
HEAD_DIM = 64
CHUNK = 64
N_MEM = 256
N_HEADS_MEM = 4
N_HEADS_TOK = 12
TOK_W = N_HEADS_TOK * HEAD_DIM
MEM_W = N_HEADS_MEM * HEAD_DIM
BAND_CHUNKS = 8
WINDOW_B = BAND_CHUNKS * CHUNK
REL_CLIP = 128
EPS = 1e-6
NEG = -1e30
QK_SCALE = HEAD_DIM ** -0.5
LOG2E = 1.4426950408889634

LANES = 128
N_PAIRS_TOK = TOK_W // LANES
N_PAIRS_MEM = MEM_W // LANES
FF_CHUNK = 256
TOKEN_TILE = 512
SB_Q_BLOCK = 256
SB_K_BLOCK = 256
SB_GROUP = 2 * LANES
SB_LANES = TOK_W
SB_EXIT = 160.0
BAND_Q_BLOCK = 4 * CHUNK
BAND_WIN = WINDOW_B + BAND_Q_BLOCK
BAND_EDGE = WINDOW_B // BAND_Q_BLOCK
VMEM_LIMIT = 56 * 1024 * 1024


def _rms(x, g):
    return x * lax.rsqrt(jnp.mean(x * x, axis=-1, keepdims=True) + EPS) * g


def _dot(a, b):
    return jnp.dot(a, b, preferred_element_type=F32)


def _dot_nt(a, b):
    return lax.dot_general(a, b, (((1,), (1,)), ((), ())), preferred_element_type=F32)


def _neg_abs(x):
    return pltpu.bitcast(pltpu.bitcast(x, jnp.uint32) | jnp.uint32(0x80000000), F32)


def _head_split(x, n_heads):
    head = jnp.right_shift(lax.broadcasted_iota(jnp.int32, x.shape, 1), HEAD_DIM.bit_length() - 1)
    zero = jnp.zeros_like(x)
    return [jnp.where(head == h, x, zero) for h in range(n_heads)]


def _head_merge(o0, o1):
    lane = lax.broadcasted_iota(jnp.int32, o0.shape, 1)
    return jnp.where(lane < HEAD_DIM, o0, o1)


def _swiglu_residual(x_ref, g_ref, wgu_ref, wd_ref, h_ref, a_ref):
    h_ref[...] = _rms(x_ref[...], g_ref[...]).astype(BF16)
    f = wd_ref.shape[0]
    for lo in range(0, f, FF_CHUNK):
        gate = _dot(h_ref[...], wgu_ref[:, lo:lo + FF_CHUNK])
        up = _dot(h_ref[...], wgu_ref[:, f + lo:f + lo + FF_CHUNK])
        a_ref[:, lo:lo + FF_CHUNK] = (gate * (1.0 / (1.0 + jnp.exp(-gate))) * up).astype(BF16)
    return x_ref[...] + 0.5 * _dot(a_ref[...], wd_ref[...])


def _softmax_pv(s, v, v_feature_major=False):
    m = jnp.max(s, axis=-1, keepdims=True)
    e = jnp.exp(s - m)
    l = jnp.sum(e, axis=-1, keepdims=True)
    pv = _dot_nt(e.astype(BF16), v) if v_feature_major else _dot(e.astype(BF16), v)
    return pv / l


def _pre_kernel(*refs, has_kv, n_alias, n_streams):
    if has_kv:
        (x_ref, gff_ref, wgu_ref, wd_ref, gmix_ref, wq_ref, wk_ref, wv_ref, wqm_ref, mk_ref, mv_ref) = refs[:11]
        xo_ref, q_ref, k_ref, v_ref, om_ref, h_ref, a_ref = refs[11 + n_alias:]
    else:
        (x_ref, gff_ref, wgu_ref, wd_ref, gmix_ref, wq_ref, wqm_ref, mk_ref, mv_ref,
         xo_ref, q_ref, om_ref, h_ref, a_ref) = refs
    xo_ref[...] = _swiglu_residual(x_ref, gff_ref, wgu_ref, wd_ref, h_ref, a_ref)
    h_ref[...] = _rms(xo_ref[...], gmix_ref[...]).astype(BF16)
    q_ref[...] = (_dot(h_ref[...], wq_ref[...]) * QK_SCALE).astype(BF16)
    if has_kv:
        k_ref[...] = _dot_nt(wk_ref[...], h_ref[...])
        v_ref[...] = _dot_nt(wv_ref[...], h_ref[...])
    qm = (_dot(h_ref[...], wqm_ref[...]) * QK_SCALE).astype(BF16)
    rows = x_ref.shape[0] // n_streams
    for s in range(n_streams):
        r0 = s * rows
        for p in range(N_PAIRS_MEM):
            c0 = p * LANES
            mk = mk_ref[s, c0:c0 + LANES, :].astype(BF16)
            mv = mv_ref[s, c0:c0 + LANES, :].astype(BF16)
            qa, qb = _head_split(qm[r0:r0 + rows, c0:c0 + LANES], 2)
            oa = _softmax_pv(_dot(qa, mk), mv, v_feature_major=True)
            ob = _softmax_pv(_dot(qb, mk), mv, v_feature_major=True)
            om_ref[r0:r0 + rows, c0:c0 + LANES] = _head_merge(oa, ob).astype(BF16)


def _const_spec(shape):
    nd = len(shape)
    return pl.BlockSpec(shape, lambda *_: (0,) * nd, pipeline_mode=pl.Buffered(1))


def _token_tile(n):
    tm = min(TOKEN_TILE, n)
    assert n % tm == 0
    return tm


def _feature_major(lead, n, seq_len, tm, width=TOK_W):
    lead_shape = tuple(s for s, _ in lead)
    lead_idx = tuple(ix for _, ix in lead)
    squeezed = (None,) * len(lead)
    if tm <= seq_len:
        tiles_per_seq = seq_len // tm
        return (lead_shape + (n // seq_len, width, seq_len),
                pl.BlockSpec(squeezed + (None, width, tm),
                             lambda i: lead_idx + (i // tiles_per_seq, 0, i % tiles_per_seq)))
    return (lead_shape + (width, n), pl.BlockSpec(squeezed + (width, tm), lambda i: lead_idx + (0, i)))


def _pre_call(x, seq_len, g_ff, wgu, wd, g_mix, w_in_parts, mem_k, mem_v, kv_slot=None):
    n = x.shape[0]
    tm = _token_tile(n)
    has_kv = kv_slot is not None
    mem_layer, mem_k = mem_k
    _, mem_v = mem_v
    if tm >= seq_len:
        assert tm % seq_len == 0
        n_streams = tm // seq_len
        mem_map = lambda i: (mem_layer, i, 0, 0)
    else:
        assert seq_len % tm == 0
        n_streams = 1
        tiles_per_seq = seq_len // tm
        mem_map = lambda i: (mem_layer, i // tiles_per_seq, 0, 0)
    row = lambda w: pl.BlockSpec((tm, w), lambda i: (i, 0))
    mem_spec = pl.BlockSpec((None, n_streams, MEM_W, N_MEM), mem_map)
    args = [x, g_ff, wgu, wd, g_mix, *w_in_parts, mem_k, mem_v]
    in_specs = [row(D_MODEL)] + [_const_spec(a.shape) for a in args[1:-2]] + [mem_spec, mem_spec]
    out_shape = [jax.ShapeDtypeStruct((n, D_MODEL), F32), jax.ShapeDtypeStruct((n, TOK_W), BF16)]
    out_specs = [row(D_MODEL), row(TOK_W)]
    aliases = {}
    n_alias = 0
    if has_kv:
        layer, n_layers, k_stack, v_stack = kv_slot
        kv_shape, kv_spec = _feature_major(((n_layers, layer),), n, seq_len, tm)
        out_shape += [jax.ShapeDtypeStruct(kv_shape, F32)] * 2
        out_specs += [kv_spec] * 2
        if k_stack is not None:
            n_alias = 2
            aliases = {len(args): 2, len(args) + 1: 3}
            args += [k_stack, v_stack]
            in_specs += [pl.BlockSpec(memory_space=pl.ANY)] * 2
    out_shape.append(jax.ShapeDtypeStruct((n, MEM_W), BF16))
    out_specs.append(row(MEM_W))
    d_ff = wd.shape[0]
    return pl.pallas_call(
        functools.partial(_pre_kernel, has_kv=has_kv, n_alias=n_alias, n_streams=n_streams),
        grid=(n // tm,),
        in_specs=in_specs,
        out_specs=out_specs,
        out_shape=out_shape,
        input_output_aliases=aliases,
        scratch_shapes=[pltpu.VMEM((tm, D_MODEL), BF16), pltpu.VMEM((tm, d_ff), BF16)],
        compiler_params=pltpu.CompilerParams(dimension_semantics=("arbitrary",),
                                             vmem_limit_bytes=VMEM_LIMIT),
        name="pre_kv" if has_kv else "pre_q",
    )(*args)


def _post_kernel(*refs, tail):
    if tail == "kv":
        (x_ref, ot_ref, om_ref, wot_ref, wom_ref, gff_ref, wgu_ref, wd_ref, gt_ref, wkb_ref, wvb_ref,
         xo_ref, kt_ref, vt_ref, k16_ref, v16_ref, h_ref, a_ref) = refs
    elif tail == "final":
        (x_ref, ot_ref, om_ref, wot_ref, wom_ref, gff_ref, wgu_ref, wd_ref, gt_ref,
         xo_ref, h_ref, a_ref) = refs
    else:
        (x_ref, ot_ref, om_ref, wot_ref, wom_ref, gff_ref, wgu_ref, wd_ref,
         xo_ref, h_ref, a_ref) = refs
    xo_ref[...] = x_ref[...] + _dot(ot_ref[...], wot_ref[...]) + _dot(om_ref[...], wom_ref[...])
    xo_ref[...] = _swiglu_residual(xo_ref, gff_ref, wgu_ref, wd_ref, h_ref, a_ref)
    if tail == "final":
        xo_ref[...] = _rms(xo_ref[...], gt_ref[...])
    if tail == "kv":
        h_ref[...] = _rms(xo_ref[...], gt_ref[...]).astype(BF16)
        kb = _dot_nt(wkb_ref[...], h_ref[...])
        vb = _dot(h_ref[...], wvb_ref[...])
        kt_ref[...] = kb
        vt_ref[...] = vb
        k16_ref[...] = kb.astype(BF16)
        v16_ref[...] = vb.astype(BF16)


def _post_call(x, seq_len, o_tok, o_mem, w_out_tok, w_out_mem, g_ff, wgu, wd, tail, tail_args):
    n = x.shape[0]
    tm = _token_tile(n)
    row = lambda w: pl.BlockSpec((tm, w), lambda i: (i, 0))
    consts = [w_out_tok, w_out_mem, g_ff, wgu, wd, *tail_args]
    in_specs = [row(D_MODEL), row(TOK_W), row(MEM_W)] + [_const_spec(c.shape) for c in consts]
    out_shape = [jax.ShapeDtypeStruct((n, D_MODEL), F32)]
    out_specs = [row(D_MODEL)]
    if tail == "kv":
        keep = min(WINDOW_B, seq_len)
        k16_shape, k16_spec = _feature_major((), n, seq_len, tm)
        if keep == seq_len:
            kt_shape, kt_spec, vt_shape, vt_spec = k16_shape, k16_spec, (n, TOK_W), row(TOK_W)
        else:
            assert keep == tm and seq_len % tm == 0
            tiles_per_seq = seq_len // tm
            kt_shape = (n // seq_len, TOK_W, keep)
            kt_spec = pl.BlockSpec((None, TOK_W, tm), lambda i: (i // tiles_per_seq, 0, 0))
            vt_shape = ((n // seq_len) * keep, TOK_W)
            vt_spec = pl.BlockSpec((tm, TOK_W), lambda i: (i // tiles_per_seq, 0))
        out_shape += [jax.ShapeDtypeStruct(kt_shape, F32), jax.ShapeDtypeStruct(vt_shape, F32),
                      jax.ShapeDtypeStruct(k16_shape, BF16), jax.ShapeDtypeStruct((n, TOK_W), BF16)]
        out_specs += [kt_spec, vt_spec, k16_spec, row(TOK_W)]
    d_ff = wd.shape[0]
    return pl.pallas_call(
        functools.partial(_post_kernel, tail=tail),
        grid=(n // tm,),
        in_specs=in_specs,
        out_specs=out_specs,
        out_shape=out_shape,
        scratch_shapes=[pltpu.VMEM((tm, D_MODEL), BF16), pltpu.VMEM((tm, d_ff), BF16)],
        compiler_params=pltpu.CompilerParams(dimension_semantics=("arbitrary",),
                                             vmem_limit_bytes=VMEM_LIMIT),
        name="post_" + tail,
    )(x, o_tok, o_mem, *consts)


def _memkv_kernel(m_ref, g_ref, wk_ref, wv_ref, k_ref, v_ref):
    h = _rms(m_ref[...], g_ref[...]).astype(BF16)
    kt = _dot_nt(wk_ref[...], h)
    vt = _dot_nt(wv_ref[...], h)
    n_mem = k_ref.shape[-1]
    for s in range(k_ref.shape[0]):
        k_ref[s] = kt[:, s * n_mem:(s + 1) * n_mem]
        v_ref[s] = vt[:, s * n_mem:(s + 1) * n_mem]


def _memkv_call(mem, g_mem, wkt, wvt):
    b, n_mem, _ = mem.shape
    depth = g_mem.shape[0]
    tm = _token_tile(b * n_mem)
    assert tm % n_mem == 0
    per_tile = tm // n_mem
    w_spec = pl.BlockSpec((None, MEM_W, D_MODEL), lambda l, i: (l, 0, 0))
    o_spec = pl.BlockSpec((None, per_tile, MEM_W, n_mem), lambda l, i: (l, i, 0, 0))
    return pl.pallas_call(
        _memkv_kernel,
        grid=(depth, b // per_tile),
        in_specs=[pl.BlockSpec((tm, D_MODEL), lambda l, i: (i, 0)),
                  pl.BlockSpec((None, 1, D_MODEL), lambda l, i: (l, 0, 0)), w_spec, w_spec],
        out_specs=[o_spec, o_spec],
        out_shape=[jax.ShapeDtypeStruct((depth, b, MEM_W, n_mem), F32)] * 2,
        compiler_params=pltpu.CompilerParams(dimension_semantics=("arbitrary", "arbitrary")),
        name="mem_kv",
    )(mem.reshape(b * n_mem, D_MODEL), g_mem, wkt, wvt)


def _sb_kernel(*refs, q_start, n_kv):
    q_ref, kv_refs = refs[0], refs[1:1 + 2 * n_kv]
    o_ref, kb_ref, vb_ref, qs_ref, acc_ref = refs[1 + 2 * n_kv:]
    i = pl.program_id(2)
    tq, width = q_ref.shape
    tk = SB_K_BLOCK
    n_heads = width // HEAD_DIM

    @pl.when(i == 0)
    def _():
        blk = 0
        for k_ref, v_ref in zip(kv_refs[0::2], kv_refs[1::2]):
            for c in range(0, k_ref.shape[-1], tk):
                kb_ref[blk] = k_ref[:, c:c + tk].astype(BF16)
                vb_ref[blk] = v_ref[:, c:c + tk].astype(BF16)
                blk += 1

    heads_per_group = SB_GROUP // HEAD_DIM
    group_rows = heads_per_group * tq
    for g in range(width // SB_GROUP):
        for h, qh in enumerate(_head_split(q_ref[:, g * SB_GROUP:(g + 1) * SB_GROUP], heads_per_group)):
            r0 = g * group_rows + h * tq
            qs_ref[r0:r0 + tq, :] = qh
    rows = n_heads * tq
    q0 = q_start + i * tq
    u_after = jnp.where(lax.broadcasted_iota(jnp.int32, (tk, tk), 0)
                        > lax.broadcasted_iota(jnp.int32, (tk, tk), 1), 1.0, 0.0).astype(BF16)
    n_blocks = jnp.minimum((q0 + tq - 2) // tk + 1, kb_ref.shape[0])

    def sweep_block(t, run, newest):
        blk = n_blocks - 1 - t
        ks = blk * tk
        z = jnp.concatenate(
            [_dot(qs_ref[g * group_rows:(g + 1) * group_rows, :], kb_ref[blk, g * SB_GROUP:(g + 1) * SB_GROUP, :])
             for g in range(width // SB_GROUP)], axis=0) * LOG2E
        sp = jnp.maximum(z, 0.0) + jnp.log2(1.0 + jnp.exp2(_neg_abs(z)))
        if newest:
            vis = (ks + lax.broadcasted_iota(jnp.int32, (tq, tk), 1)
                   < q0 + lax.broadcasted_iota(jnp.int32, (tq, tk), 0))[None]
            visible = lambda a: jnp.where(vis, a.reshape(n_heads, tq, tk), 0.0).reshape(rows, tk)
            sp = visible(sp)
        sp16 = sp.astype(BF16)
        after = _dot(sp16, u_after)
        if newest:
            w = visible(jnp.exp2((z - sp) - after))
        else:
            w = jnp.exp2((z - sp) - after - run)
        w = w.astype(BF16)
        for p in range(n_heads // 2):
            pair_rows = slice(2 * p * tq, (2 * p + 2) * tq)
            pv = _dot_nt(w[pair_rows], vb_ref[blk, p * LANES:(p + 1) * LANES, :])
            if newest:
                acc_ref[pair_rows] = pv
            else:
                acc_ref[pair_rows] += pv
        return run + after[:, 0:1] + sp16[:, 0:1].astype(F32)

    def unfinished(run):
        return (jnp.min(run) < SB_EXIT).astype(jnp.int32)

    def cond(carry):
        t, go, _ = carry
        return jnp.logical_and(t < n_blocks, go > 0)

    def body(carry):
        t, _, run = carry
        run = sweep_block(t, run, False)
        return t + 1, unfinished(run), run

    run = sweep_block(0, jnp.zeros((rows, 1), F32), True)
    lax.while_loop(cond, body, (jnp.int32(1), unfinished(run), run))
    for p in range(n_heads // 2):
        r0 = 2 * p * tq
        o_ref[:, p * LANES:(p + 1) * LANES] = _head_merge(
            acc_ref[r0:r0 + tq], acc_ref[r0 + tq:r0 + 2 * tq]).astype(BF16)


def _sb_call(q, kv_pairs, q_start):
    b, t, _ = q.shape
    tq = min(SB_Q_BLOCK, t)
    tk_all = sum(k.shape[-1] for k, _, _ in kv_pairs)
    assert t % tq == 0 and SB_K_BLOCK % tq == 0 and q_start % tq == 0
    q_spec = pl.BlockSpec((None, tq, SB_LANES), lambda bi, g, i: (bi, i, g))
    kv_args, kv_specs = [], []
    for k, v, layer in kv_pairs:
        assert k.shape[-1] % SB_K_BLOCK == 0
        if layer is None:
            spec = pl.BlockSpec((None, SB_LANES, k.shape[-1]), lambda bi, g, i: (bi, g, 0))
        else:
            spec = pl.BlockSpec((None, None, SB_LANES, k.shape[-1]),
                                lambda bi, g, i, layer=layer: (layer, bi, g, 0))
        kv_args += [k, v]
        kv_specs += [spec, spec]
    kv_scratch = pltpu.VMEM((tk_all // SB_K_BLOCK, SB_LANES, SB_K_BLOCK), BF16)
    return pl.pallas_call(
        functools.partial(_sb_kernel, q_start=q_start, n_kv=len(kv_pairs)),
        grid=(b, TOK_W // SB_LANES, t // tq),
        in_specs=[q_spec] + kv_specs,
        out_specs=q_spec,
        out_shape=jax.ShapeDtypeStruct((b, t, TOK_W), BF16),
        scratch_shapes=[kv_scratch, kv_scratch,
                        pltpu.VMEM((SB_LANES // HEAD_DIM * tq, SB_GROUP), BF16),
                        pltpu.VMEM((SB_LANES // HEAD_DIM * tq, LANES), F32)],
        compiler_params=pltpu.CompilerParams(dimension_semantics=("arbitrary",) * 3,
                                             vmem_limit_bytes=VMEM_LIMIT),
        name="stick_breaking",
    )(q, *kv_args)


def _toeplitz(line, rows, cols):
    period = rows + cols
    e = jnp.roll(jnp.pad(line, ((0, 0), (0, 1))), -(rows - 1), axis=1)
    tiled = jnp.tile(e, (1, rows))[:, :rows * (period - 1)]
    return tiled.reshape(line.shape[0], rows, period - 1)[:, :, :cols]


def _band_line(rel_bias, dist0, length):
    dist = dist0 - np.arange(length)
    return rel_bias[:, np.clip(dist, -REL_CLIP, REL_CLIP) + REL_CLIP]


def _band_visible(q_pos0, n_q, k_pos0, n_k, k_first):
    q_pos = q_pos0 + np.arange(n_q)
    k_pos = k_pos0 + np.arange(n_k)
    qc = q_pos[:, None] // CHUNK
    kc = k_pos[None, :] // CHUNK
    return (k_pos[None, :] >= k_first) & (kc <= qc) & (kc >= qc - BAND_CHUNKS)


def _band_heads(q_ref, kt_ref, v_ref, bm_ref, o_ref, keys, bias_cols):
    tq = q_ref.shape[0]
    for p in range(q_ref.shape[-1] // LANES):
        lanes = slice(p * LANES, (p + 1) * LANES)
        qs = jnp.concatenate(_head_split(q_ref[:, lanes], 2), axis=0)
        bias = bm_ref[2 * p:2 * p + 2, :, bias_cols].reshape(2 * tq, -1)
        o = _softmax_pv(_dot(qs, kt_ref[lanes, keys]) + bias, v_ref[keys, lanes])
        o_ref[:, lanes] = _head_merge(o[:tq], o[tq:]).astype(BF16)


def _band_prompt_kernel(q_ref, kt_ref, v_ref, line_ref, o_ref, bm_ref):
    tq = BAND_Q_BLOCK
    n_cols = bm_ref.shape[-1]

    @pl.when(jnp.logical_and(pl.program_id(0) == 0, pl.program_id(1) == 0))
    def _():
        chunk_shift = CHUNK.bit_length() - 1
        r_chunk = jnp.right_shift(lax.broadcasted_iota(jnp.int32, (tq, n_cols), 0), chunk_shift)
        k_chunk = jnp.right_shift(lax.broadcasted_iota(jnp.int32, (tq, n_cols), 1), chunk_shift)
        vis = jnp.logical_and(k_chunk >= r_chunk, k_chunk <= r_chunk + BAND_CHUNKS)
        for h in range(bm_ref.shape[0]):
            line = jnp.broadcast_to(line_ref[h], (tq, line_ref.shape[-1]))
            rows = pltpu.roll(line, 0, 1, stride=1, stride_axis=0)
            bm_ref[h] = jnp.where(vis, rows[:, :n_cols], NEG)

    edge = jnp.minimum(pl.program_id(1), BAND_EDGE) * tq
    start = pl.multiple_of(pl.program_id(1) * tq - edge, tq)
    first = pl.multiple_of(WINDOW_B - edge, tq)
    _band_heads(q_ref, kt_ref, v_ref, bm_ref, o_ref, pl.ds(start, BAND_WIN), pl.ds(first, BAND_WIN))


def _band_prompt_call(q, kt, v, rel_bias):
    b, t, _ = q.shape
    tq = BAND_Q_BLOCK
    assert t % tq == 0 and t >= BAND_WIN and WINDOW_B % tq == 0
    n_cols = BAND_WIN + BAND_EDGE * tq
    period = n_cols + tq
    dist = WINDOW_B - (np.arange(period) + tq - 1) % period + tq - 1
    line = rel_bias[:, np.clip(dist, -REL_CLIP, REL_CLIP) + REL_CLIP].reshape(-1, 1, period)
    q_spec = pl.BlockSpec((None, tq, TOK_W), lambda bi, i: (bi, i, 0))
    return pl.pallas_call(
        _band_prompt_kernel,
        grid=(b, t // tq),
        in_specs=[q_spec, pl.BlockSpec((None, TOK_W, t), lambda bi, i: (bi, 0, 0)),
                  pl.BlockSpec((None, t, TOK_W), lambda bi, i: (bi, 0, 0)), _const_spec(line.shape)],
        out_specs=q_spec,
        out_shape=jax.ShapeDtypeStruct((b, t, TOK_W), BF16),
        scratch_shapes=[pltpu.VMEM((N_HEADS_TOK, tq, n_cols), F32)],
        compiler_params=pltpu.CompilerParams(dimension_semantics=("arbitrary",) * 2,
                                             vmem_limit_bytes=VMEM_LIMIT),
        name="band_prompt",
    )(q, kt, v, line)


def _band_step_kernel(q_ref, kt_ref, v_ref, bm_ref, o_ref):
    _band_heads(q_ref, kt_ref, v_ref, bm_ref, o_ref, slice(None), slice(None))


def _band_step_call(q, kt_all, v_all, rel_bias, q_start):
    b, t, _ = q.shape
    k_first = q_start + t - v_all.shape[1]
    k_pad = (-v_all.shape[1]) % LANES
    kt_all = jnp.pad(kt_all, ((0, 0), (0, 0), (k_pad, 0)))
    v_all = jnp.pad(v_all, ((0, 0), (k_pad, 0), (0, 0)))
    tk = v_all.shape[1]
    k_pos0 = k_first - k_pad
    table = _toeplitz(_band_line(rel_bias, q_start - k_pos0 + t - 1, t + tk - 1), t, tk)
    bm = jnp.where(_band_visible(q_start, t, k_pos0, tk, k_first)[None], table, NEG)
    q_spec = pl.BlockSpec((None, t, TOK_W), lambda bi: (bi, 0, 0))
    return pl.pallas_call(
        _band_step_kernel,
        grid=(b,),
        in_specs=[q_spec, pl.BlockSpec((None, TOK_W, tk), lambda bi: (bi, 0, 0)),
                  pl.BlockSpec((None, tk, TOK_W), lambda bi: (bi, 0, 0)), _const_spec(bm.shape)],
        out_specs=q_spec,
        out_shape=jax.ShapeDtypeStruct((b, t, TOK_W), BF16),
        compiler_params=pltpu.CompilerParams(dimension_semantics=("arbitrary",)),
        name="band_step",
    )(q, kt_all, v_all, bm)


def _split_cols(w, widths):
    out, c = [], 0
    for wd in widths:
        out.append(w[:, c:c + wd].astype(BF16))
        c += wd
    return out


def _prep_layers(params):
    depth = params["g_ff1"].shape[0]
    n_a = params["w_in_a"].shape[0]
    layers = []
    for l in range(depth):
        lay = {name: params[name][l].reshape(1, D_MODEL) for name in ("g_ff1", "g_mix", "g_ff2")}
        lay["ff1"] = (params["w_ff1_gu"][l].astype(BF16), params["w_ff1_down"][l].astype(BF16))
        lay["ff2"] = (params["w_ff2_gu"][l].astype(BF16), params["w_ff2_down"][l].astype(BF16))
        if l < n_a:
            wq, wk, wv, wqm = _split_cols(params["w_in_a"][l], (TOK_W, TOK_W, TOK_W, MEM_W))
            lay["w_in"] = [wq, wk.T, wv.T, wqm]
        else:
            lay["w_in"] = _split_cols(params["w_in_b"][l - n_a], (TOK_W, MEM_W))
            lay["rel_bias"] = params["rel_bias_b"][l - n_a]
        lay["w_out"] = (params["w_out"][l][:TOK_W].astype(BF16), params["w_out"][l][TOK_W:].astype(BF16))
        if l == n_a - 1:
            lay["tail"] = "kv"
            wkb, wvb = _split_cols(params["w_kv_b"], (TOK_W, TOK_W))
            lay["tail_args"] = [params["g_kv"].reshape(1, D_MODEL), wkb.T, wvb]
        elif l == depth - 1:
            lay["tail"], lay["tail_args"] = "final", [params["g_final"].reshape(1, D_MODEL)]
        else:
            lay["tail"], lay["tail_args"] = "none", []
        layers.append(lay)
    return layers, n_a


def _per_stream(a, bn, t):
    if a.shape[-1] == t:
        return a
    return jnp.moveaxis(a.reshape(a.shape[:-1] + (bn, t)), -2, -3)


def _rows_to_features(a):
    return jnp.swapaxes(a.reshape(a.shape[:-2] + (a.shape[-2] * a.shape[-1],)), -1, -2)


def _features_to_heads(a):
    a = a.reshape(a.shape[:-2] + (a.shape[-2] // HEAD_DIM, HEAD_DIM, a.shape[-1]))
    return jnp.moveaxis(a, -1, -3)


def _run(x, mem_k, mem_v, past_a_k, past_a_v, past_b_k, past_b_v, layers, n_a):
    bn, t, _ = x.shape
    n = bn * t
    has_past = past_a_k is not None
    q_start = past_a_k.shape[2] if has_past else 0
    x = x.reshape(n, D_MODEL)
    k_stack = v_stack = kb_tail = vb_tail = kb16 = vb16 = None
    for l, lay in enumerate(layers):
        pre_args = (lay["g_ff1"], *lay["ff1"], lay["g_mix"], lay["w_in"], (l, mem_k), (l, mem_v))
        if l < n_a:
            x, q, k_stack, v_stack, o_mem = _pre_call(x, t, *pre_args, kv_slot=(l, n_a, k_stack, v_stack))
            q = q.reshape(bn, t, TOK_W)
            if has_past:
                pad = ((0, 0), (0, 0), (0, (-t) % SB_K_BLOCK))
                new = [jnp.pad(_per_stream(s[l], bn, t), pad) for s in (k_stack, v_stack)]
                o_tok = _sb_call(q, [(_rows_to_features(past_a_k), _rows_to_features(past_a_v), l),
                                     (new[0], new[1], None)], q_start)
            else:
                o_tok = _sb_call(q, [(k_stack, v_stack, l)], q_start)
        else:
            x, q, o_mem = _pre_call(x, t, *pre_args)
            q = q.reshape(bn, t, TOK_W)
            kt16 = _per_stream(kb16, bn, t)
            v16 = vb16.reshape(bn, t, TOK_W)
            if has_past:
                kt_all = jnp.concatenate([_rows_to_features(past_b_k).astype(BF16), kt16], axis=2)
                v_all = jnp.concatenate([past_b_v.reshape(bn, -1, TOK_W).astype(BF16), v16], axis=1)
                o_tok = _band_step_call(q, kt_all, v_all, lay["rel_bias"], q_start)
            else:
                o_tok = _band_prompt_call(q, kt16, v16, lay["rel_bias"])
        outs = _post_call(x, t, o_tok.reshape(n, TOK_W), o_mem, *lay["w_out"], lay["g_ff2"], *lay["ff2"],
                          lay["tail"], lay["tail_args"])
        x = outs[0]
        if lay["tail"] == "kv":
            _, kb_tail, vb_tail, kb16, vb16 = outs
    return (x.reshape(bn, t, D_MODEL),
            _features_to_heads(_per_stream(k_stack, bn, t)), _features_to_heads(_per_stream(v_stack, bn, t)),
            _features_to_heads(_per_stream(kb_tail, bn, min(WINDOW_B, t))),
            vb_tail.reshape(bn, -1, N_HEADS_TOK, HEAD_DIM))


def kernel(x_prompt, x_sample, cache_a_k, cache_a_v, cache_b_k, cache_b_v, cache_mem_k, cache_mem_v,
           mem_prompt, g_ff1, w_ff1_gu, w_ff1_down, g_mix, w_in_a, w_in_b, w_out, g_mem, w_mem_kv,
           g_kv, w_kv_b, rel_bias_b, g_ff2, w_ff2_gu, w_ff2_down, g_final):
    params = dict(g_ff1=g_ff1, w_ff1_gu=w_ff1_gu, w_ff1_down=w_ff1_down, g_mix=g_mix, w_in_a=w_in_a,
                  w_in_b=w_in_b, w_out=w_out, g_kv=g_kv, w_kv_b=w_kv_b, rel_bias_b=rel_bias_b,
                  g_ff2=g_ff2, w_ff2_gu=w_ff2_gu, w_ff2_down=w_ff2_down, g_final=g_final)
    depth = g_mem.shape[0]
    bn, n_mem, _ = mem_prompt.shape

    w_mem_t = jnp.swapaxes(w_mem_kv, 1, 2).astype(BF16)
    mk, mv = _memkv_call(mem_prompt, g_mem.reshape(depth, 1, D_MODEL), w_mem_t[:, :MEM_W], w_mem_t[:, MEM_W:])

    layers, n_a = _prep_layers(params)
    y_prompt, a_k_prompt, a_v_prompt, b_k_prompt, b_v_prompt = _run(
        x_prompt, mk, mv, None, None, None, None, layers, n_a)

    y_sample, a_k_sample, a_v_sample, b_k_sample, b_v_sample = _run(
        x_sample, _rows_to_features(cache_mem_k), _rows_to_features(cache_mem_v),
        cache_a_k, cache_a_v, cache_b_k, cache_b_v, layers, n_a)

    return (y_prompt, y_sample, a_k_prompt, a_v_prompt, b_k_prompt, b_v_prompt,
            _features_to_heads(mk), _features_to_heads(mv), a_k_sample, a_v_sample, b_k_sample, b_v_sample)
```
